```python
import jax, jax.numpy as jnp
from jax import lax
import numpy as np

D_MODEL = 1024
BATCH = 8
SEQ = 2048
DEPTH = 1
DEC_BATCH = 128
DEC_SEQ = 1
PAST_LEN = 16384
PAGE_SIZE = 128

HEAD_DIM = 64
D_RWKV = D_MODEL
N_HEADS = D_RWKV // HEAD_DIM
D_DECAY_LORA = 64
D_AAA_LORA = 64
D_GATE_LORA = 128
GN_EPS = 64e-5
D_CONV = D_MODEL
CONV_WIDTH = 31
LN_EPS = 1e-5
N_EXPERTS = 64
N_GROUPS = 8
TOPK_GROUPS = 4
TOP_K = 8
D_EXPERT = 256
D_SHARED = 256
ROUTED_SCALE = 2.5
MOE_BLOCK = 256
RMS_EPS = 1e-6

O_R = 0
O_K = O_R + D_RWKV
O_V = O_K + D_RWKV
O_W = O_V + D_RWKV
O_A = O_W + D_DECAY_LORA
O_G = O_A + D_AAA_LORA
C_RWKV = O_G + D_GATE_LORA
O_GLU = C_RWKV
O_MIX = O_GLU + 2 * D_CONV
C_IN = O_MIX + 2 * D_MODEL

kernel_name = 'rwkv7_conformer_conv_moe_hybrid_step'


def _rmsnorm(x, g):
    xf = x.astype(jnp.float32)
    y = xf * lax.rsqrt(jnp.mean(xf * xf, axis=-1, keepdims=True) + RMS_EPS) * g.astype(jnp.float32)
    return y.astype(x.dtype)


def _layernorm(xf, g, b):
    mu = jnp.mean(xf, axis=-1, keepdims=True)
    var = jnp.mean(jnp.square(xf - mu), axis=-1, keepdims=True)
    return (xf - mu) * lax.rsqrt(var + LN_EPS) * g.astype(jnp.float32) + b.astype(jnp.float32)


def _wkv_scan(s0, r, decay, k, v, kk, b):
    def step(S, inp):
        r_t, d_t, k_t, v_t, kk_t, b_t = inp
        sa = jnp.einsum('bhvk,bhk->bhv', S, -kk_t)
        S = S * d_t[:, :, None, :] + sa[..., None] * b_t[:, :, None, :] + v_t[..., None] * k_t[:, :, None, :]
        return S, jnp.einsum('bhvk,bhk->bhv', S, r_t)
    seq = tuple(jnp.moveaxis(t, 1, 0) for t in (r, decay, k, v, kk, b))
    s_new, o = lax.scan(step, s0, seq)
    return s_new, jnp.moveaxis(o, 0, 1)


def _mixers(xn, s_wkv, s_shift, s_conv, p):
    f32 = jnp.float32
    B, T, _ = xn.shape
    z = xn @ p['w_in']
    zr = z[..., :C_RWKV]
    zr_prev0 = (s_shift.astype(xn.dtype) @ p['w_in'][:, :C_RWKV]).astype(zr.dtype)
    zr_prev = jnp.concatenate([zr_prev0[:, None, :], zr[:, :-1]], axis=1)
    zm = (zr + (zr_prev - zr) * p['mu_shift']).astype(f32)
    r = zm[..., O_R:O_K]
    k = zm[..., O_K:O_V]
    v = zm[..., O_V:O_W]
    w_lora = jnp.tanh(zm[..., O_W:O_A]) @ p['w_decay_up'].astype(f32)
    w_log = -jax.nn.softplus(-(p['w0'].astype(f32) + w_lora)) - 0.5
    decay = jnp.exp(-jnp.exp(w_log))
    a = jax.nn.sigmoid(p['a0'].astype(f32) + zm[..., O_A:O_G] @ p['a_up'].astype(f32))
    g = jax.nn.sigmoid(zm[..., O_G:C_RWKV]) @ p['g_up'].astype(f32)
    hd = lambda t: t.reshape(B, T, N_HEADS, HEAD_DIM)
    kk = hd(k * p['k_k'].astype(f32))
    kk = kk / jnp.maximum(jnp.sqrt(jnp.sum(kk * kk, axis=-1, keepdims=True)), 1e-12)
    k = k * (1.0 + (a - 1.0) * p['k_a'].astype(f32))
    r, k, v, a, decay = hd(r), hd(k), hd(v), hd(a), hd(decay)
    s_wkv_new, o = _wkv_scan(s_wkv.astype(f32), r, decay, k, v, kk, kk * a)
    mu = jnp.mean(o, axis=-1, keepdims=True)
    var = jnp.mean(jnp.square(o - mu), axis=-1, keepdims=True)
    o = ((o - mu) * lax.rsqrt(var + GN_EPS)).reshape(B, T, D_RWKV)
    o = o * p['gn_g'].astype(f32) + p['gn_b'].astype(f32)
    bonus = (jnp.sum(r * k * p['r_k'].astype(f32), axis=-1, keepdims=True) * v).reshape(B, T, D_RWKV)
    out_a = (o + bonus) * g
    glu = z[..., O_GLU:O_MIX]
    u = glu[..., :D_CONV] * jax.nn.sigmoid(glu[..., D_CONV:])
    u_full = jnp.concatenate([s_conv.astype(u.dtype), u], axis=1)
    c = lax.conv_general_dilated(u_full, p['conv_w'][:, None, :].astype(u.dtype),
                                 window_strides=(1,), padding='VALID',
                                 dimension_numbers=('NWC', 'WIO', 'NWC'),
                                 feature_group_count=D_CONV)
    c = c.astype(f32) + p['conv_b'].astype(f32)
    out_b = jax.nn.silu(_layernorm(c, p['cln_g'], p['cln_b']))
    gates = jax.nn.sigmoid(z[..., O_MIX:].astype(f32))
    merged = gates[..., :D_MODEL] * out_a + gates[..., D_MODEL:] * out_b
    y = merged.astype(xn.dtype) @ p['w_out']
    return y, s_wkv_new, xn[:, -1], u_full[:, -(CONV_WIDTH - 1):]


def _moe(x, p):
    B, T, D = x.shape
    xf = x.reshape(B * T, D)
    n = xf.shape[0]
    scores = jax.nn.sigmoid(xf.astype(jnp.float32) @ p['w_router'].astype(jnp.float32))
    biased = scores + p['e_bias'].astype(jnp.float32)
    grp = biased.reshape(n, N_GROUPS, N_EXPERTS // N_GROUPS)
    grp_score = jnp.sum(lax.top_k(grp, 2)[0], axis=-1)
    _, gidx = lax.top_k(grp_score, TOPK_GROUPS)
    gmask = jnp.sum(jax.nn.one_hot(gidx, N_GROUPS, dtype=jnp.float32), axis=1) > 0
    emask = jnp.repeat(gmask, N_EXPERTS // N_GROUPS, axis=1)
    _, eidx = lax.top_k(jnp.where(emask, biased, -jnp.inf), TOP_K)
    wsel = jnp.take_along_axis(scores, eidx, axis=-1)
    wsel = wsel / jnp.sum(wsel, axis=-1, keepdims=True) * ROUTED_SCALE
    gate = jnp.sum(jax.nn.one_hot(eidx, N_EXPERTS, dtype=jnp.float32) * wsel[..., None], axis=1)
    n_pad = (-n) % MOE_BLOCK
    xp = jnp.pad(xf, ((0, n_pad), (0, 0))).reshape(-1, MOE_BLOCK, D)
    gp = jnp.pad(gate, ((0, n_pad), (0, 0))).reshape(-1, MOE_BLOCK, N_EXPERTS)
    w_gate, w_up, w_down = p['w_gate'], p['w_up'], p['w_down']

    def expert_block(args):
        xb, gb = args
        h = jax.nn.silu(jnp.einsum('td,edf->tef', xb, w_gate)) * jnp.einsum('td,edf->tef', xb, w_up)
        h = h * gb[..., None].astype(h.dtype)
        return jnp.einsum('tef,efd->td', h, w_down)

    routed = lax.map(expert_block, (xp, gp)).reshape(-1, D)[:n]
    shared = (jax.nn.silu(xf @ p['ws_gate']) * (xf @ p['ws_up'])) @ p['ws_down']
    return (routed + shared).astype(x.dtype).reshape(B, T, D)


def _layer(x, s_wkv, s_shift, s_conv, p):
    mix, s_wkv_new, s_shift_new, s_conv_new = _mixers(_rmsnorm(x, p['norm1_g']), s_wkv, s_shift, s_conv, p)
    h = x + mix
    y = h + _moe(_rmsnorm(h, p['norm2_g']), p)
    return y, s_wkv_new, s_shift_new, s_conv_new


def setup_inputs(seed: int = 0) -> dict:
    key = jax.random.key(seed)
    ks = jax.random.split(key, 40)
    f32 = jnp.float32
    nrm = lambda k, shape, s: jax.random.normal(k, shape, f32) * s
    L = DEPTH
    return {
        'x_prompt': nrm(ks[0], (BATCH, SEQ, D_MODEL), 1.0),
        'x_sample': nrm(ks[1], (DEC_BATCH, DEC_SEQ, D_MODEL), 1.0),
        'state_wkv': nrm(ks[2], (L, DEC_BATCH, N_HEADS, HEAD_DIM, HEAD_DIM), 0.3),
        'state_shift': nrm(ks[3], (L, DEC_BATCH, D_MODEL), 1.0),
        'state_conv': nrm(ks[4], (L, DEC_BATCH, CONV_WIDTH - 1, D_CONV), 0.5),
        'norm1_g': 1.0 + nrm(ks[5], (L, D_MODEL), 0.02),
        'w_in': nrm(ks[6], (L, D_MODEL, C_IN), D_MODEL ** -0.5),
        'mu_shift': jax.random.uniform(ks[7], (L, C_RWKV), f32),
        'w0': nrm(ks[8], (L, D_RWKV), 1.0),
        'w_decay_up': nrm(ks[9], (L, D_DECAY_LORA, D_RWKV), 0.5 * D_DECAY_LORA ** -0.5),
        'a0': nrm(ks[10], (L, D_RWKV), 0.5),
        'a_up': nrm(ks[11], (L, D_AAA_LORA, D_RWKV), 0.5 * D_AAA_LORA ** -0.5),
        'g_up': nrm(ks[12], (L, D_GATE_LORA, D_RWKV), D_GATE_LORA ** -0.5),
        'k_k': 0.85 + nrm(ks[13], (L, D_RWKV), 0.05),
        'k_a': 1.0 + nrm(ks[14], (L, D_RWKV), 0.05),
        'r_k': nrm(ks[15], (L, N_HEADS, HEAD_DIM), 0.1),
        'gn_g': 1.0 + nrm(ks[16], (L, D_RWKV), 0.02),
        'gn_b': nrm(ks[17], (L, D_RWKV), 0.02),
        'conv_w': nrm(ks[18], (L, CONV_WIDTH, D_CONV), CONV_WIDTH ** -0.5),
        'conv_b': nrm(ks[19], (L, D_CONV), 0.02),
        'cln_g': 1.0 + nrm(ks[20], (L, D_CONV), 0.02),
        'cln_b': nrm(ks[21], (L, D_CONV), 0.02),
        'w_out': nrm(ks[22], (L, D_MODEL, D_MODEL), 0.5 * D_MODEL ** -0.5),
        'norm2_g': 1.0 + nrm(ks[23], (L, D_MODEL), 0.02),
        'w_router': nrm(ks[24], (L, D_MODEL, N_EXPERTS), D_MODEL ** -0.5),
        'e_bias': nrm(ks[25], (L, N_EXPERTS), 0.01),
        'w_gate': nrm(ks[26], (L, N_EXPERTS, D_MODEL, D_EXPERT), D_MODEL ** -0.5),
        'w_up': nrm(ks[27], (L, N_EXPERTS, D_MODEL, D_EXPERT), D_MODEL ** -0.5),
        'w_down': nrm(ks[28], (L, N_EXPERTS, D_EXPERT, D_MODEL), D_EXPERT ** -0.5),
        'ws_gate': nrm(ks[29], (L, D_MODEL, D_SHARED), D_MODEL ** -0.5),
        'ws_up': nrm(ks[30], (L, D_MODEL, D_SHARED), D_MODEL ** -0.5),
        'ws_down': nrm(ks[31], (L, D_SHARED, D_MODEL), D_SHARED ** -0.5),
        'normf_g': 1.0 + nrm(ks[32], (D_MODEL,), 0.02),
    }


def reference(x_prompt, x_sample, state_wkv, state_shift, state_conv, norm1_g, w_in, mu_shift, w0,
              w_decay_up, a0, a_up, g_up, k_k, k_a, r_k, gn_g, gn_b, conv_w, conv_b, cln_g, cln_b,
              w_out, norm2_g, w_router, e_bias, w_gate, w_up, w_down, ws_gate, ws_up, ws_down, normf_g):
    hp, hs = x_prompt, x_sample
    wkv_p, shift_p, conv_p, wkv_s, shift_s, conv_s = [], [], [], [], [], []
    for l in range(DEPTH):
        p = {'norm1_g': norm1_g[l], 'w_in': w_in[l], 'mu_shift': mu_shift[l], 'w0': w0[l],
             'w_decay_up': w_decay_up[l], 'a0': a0[l], 'a_up': a_up[l], 'g_up': g_up[l],
             'k_k': k_k[l], 'k_a': k_a[l], 'r_k': r_k[l], 'gn_g': gn_g[l], 'gn_b': gn_b[l],
             'conv_w': conv_w[l], 'conv_b': conv_b[l], 'cln_g': cln_g[l], 'cln_b': cln_b[l],
             'w_out': w_out[l], 'norm2_g': norm2_g[l], 'w_router': w_router[l], 'e_bias': e_bias[l],
             'w_gate': w_gate[l], 'w_up': w_up[l], 'w_down': w_down[l],
             'ws_gate': ws_gate[l], 'ws_up': ws_up[l], 'ws_down': ws_down[l]}
        z_wkv = jnp.zeros((hp.shape[0], N_HEADS, HEAD_DIM, HEAD_DIM), jnp.float32)
        z_shift = jnp.zeros((hp.shape[0], D_MODEL), hp.dtype)
        z_conv = jnp.zeros((hp.shape[0], CONV_WIDTH - 1, D_CONV), hp.dtype)
        hp, a1, a2, a3 = _layer(hp, z_wkv, z_shift, z_conv, p)
        hs, b1, b2, b3 = _layer(hs, state_wkv[l], state_shift[l], state_conv[l], p)
        wkv_p.append(a1.astype(state_wkv.dtype))
        shift_p.append(a2.astype(state_shift.dtype))
        conv_p.append(a3.astype(state_conv.dtype))
        wkv_s.append(b1.astype(state_wkv.dtype))
        shift_s.append(b2.astype(state_shift.dtype))
        conv_s.append(b3.astype(state_conv.dtype))
    y_prompt = _rmsnorm(hp, normf_g)
    y_sample = _rmsnorm(hs, normf_g)
    return (y_prompt, y_sample, jnp.stack(wkv_p), jnp.stack(shift_p), jnp.stack(conv_p),
            jnp.stack(wkv_s), jnp.stack(shift_s), jnp.stack(conv_s))
```

```python
import functools

import jax
import jax.numpy as jnp
from jax import lax
from jax.experimental import pallas as pl
from jax.experimental.pallas import tpu as pltpu

F32 = jnp.float32
BF16 = jnp.bfloat16

D_MODEL = 1024
HEAD_DIM = 64
N_HEADS = D_MODEL // HEAD_DIM
D_DECAY_LORA = 64
D_AAA_LORA = 64
D_GATE_LORA = 128
GN_EPS = 64e-5
CONV_WIDTH = 31
LN_EPS = 1e-5
N_EXPERTS = 64
N_GROUPS = 8
TOPK_GROUPS = 4
TOP_K = 8
D_EXPERT = 256
ROUTED_SCALE = 2.5
RMS_EPS = 1e-6

O_K = D_MODEL
O_V = 2 * D_MODEL
O_W = 3 * D_MODEL
O_A = O_W + D_DECAY_LORA
O_G = O_A + D_AAA_LORA
C_RWKV = O_G + D_GATE_LORA
C_REST = 4 * D_MODEL

LANES = 128
CHUNK = 64
PAIR = 2 * HEAD_DIM
VMEM_LIMIT = 56 * 1024 * 1024

NN = ((1,), (0,))
NT = ((1,), (1,))
TN = ((0,), (0,))


def _dg(a, b, dims):
    return lax.dot_general(a, b, (dims, ((), ())), preferred_element_type=F32)


def _split2(x):
    hi = x.astype(BF16)
    lo = (x - hi.astype(F32)).astype(BF16)
    return hi, lo


def _split3(x):
    hi = x.astype(BF16)
    r1 = x - hi.astype(F32)
    mid = r1.astype(BF16)
    lo = (r1 - mid.astype(F32)).astype(BF16)
    return hi, mid, lo


def _mm1(a, b, dims=NN):
    return _dg(a.astype(BF16), b.astype(BF16), dims)


def _mm3(a, b, dims=NN):
    ah, al = _split2(a)
    bh, bl = _split2(b)
    return _dg(ah, bh, dims) + (_dg(ah, bl, dims) + _dg(al, bh, dims))


def _mm_exact_rhs(a, b_bf16, dims=NN):
    h, m, l = _split3(a)
    return _dg(h, b_bf16, dims) + (_dg(m, b_bf16, dims) + _dg(l, b_bf16, dims))


def _rmsnorm(x, g):
    return x * lax.rsqrt(jnp.mean(x * x, axis=-1, keepdims=True) + RMS_EPS) * g


def _sigmoid(x):
    return 1.0 / (1.0 + jnp.exp(-x))


def _seg_mats():
    row = lax.broadcasted_iota(jnp.int32, (D_MODEL, LANES), 0) // HEAD_DIM
    col = lax.broadcasted_iota(jnp.int32, (D_MODEL, LANES), 1)
    seg = (row == col).astype(BF16)
    rowt = lax.broadcasted_iota(jnp.int32, (LANES, D_MODEL), 0)
    colt = lax.broadcasted_iota(jnp.int32, (LANES, D_MODEL), 1) // HEAD_DIM
    exp = (rowt == colt).astype(BF16)
    return seg, exp


def _cparams(sem):
    return pltpu.CompilerParams(dimension_semantics=sem, vmem_limit_bytes=VMEM_LIMIT)


def _full(shape):
    n = len(shape)
    return pl.BlockSpec(shape, lambda *_: (0,) * n)


def _prep_math(zr, zp, mu, w0, wdu, a0, aup, gup, kk_w, ka_w, rk_w, seg, exp):
    zm = zr + (zp - zr) * mu
    r = zm[:, 0:O_K]
    k = zm[:, O_K:O_V]
    v = zm[:, O_V:O_W]
    xw = jnp.tanh(zm[:, O_W:O_A])
    xa = zm[:, O_A:O_G]
    xg = _sigmoid(zm[:, O_G:C_RWKV])
    y = -(w0 + _mm3(xw, wdu))
    softplus = jnp.maximum(y, 0.0) + jnp.log(1.0 + jnp.exp(-jnp.abs(y)))
    lw = -jnp.exp(-softplus - 0.5)
    a = _sigmoid(a0 + _mm3(xa, aup))
    g = _mm3(xg, gup)
    kkr = k * kk_w
    ss = _mm_exact_rhs(kkr * kkr, seg)
    inv = 1.0 / jnp.maximum(jnp.sqrt(ss), 1e-12)
    kk = kkr * _mm_exact_rhs(inv, exp)
    kf = k * (1.0 + (a - 1.0) * ka_w)
    b = kk * a
    rk = _mm_exact_rhs(r * kf * rk_w, seg)
    bonus = _mm_exact_rhs(rk, exp) * v
    return r, lw, kf, v, kk, b, g, bonus


def _prep_seq_kernel(x_ref, zp0_ref, g1_ref, w_ref, mu_ref, w0_ref, wdu_ref, a0_ref, aup_ref, gup_ref,
                     kkw_ref, kaw_ref, rkw_ref,
                     r_ref, lw_ref, k_ref, v_ref, kk_ref, b_ref, g_ref, bo_ref, xl_ref, carry_ref):
    t = pl.program_id(1)
    tm = x_ref.shape[1]

    @pl.when(t == 0)
    def _():
        carry_ref[...] = zp0_ref[0]

    xn = _rmsnorm(x_ref[0], g1_ref[...])
    zr = _dg(xn.astype(BF16), w_ref[...], NN)
    rows = lax.broadcasted_iota(jnp.int32, zr.shape, 0)
    zp = jnp.where(rows == 0, carry_ref[...], pltpu.roll(zr, 1, 0))
    carry_ref[...] = zr[tm - 1:tm, :]
    seg, exp = _seg_mats()
    outs = _prep_math(zr, zp, mu_ref[...], w0_ref[...], wdu_ref[...], a0_ref[...], aup_ref[...], gup_ref[...],
                      kkw_ref[...], kaw_ref[...], rkw_ref[...], seg, exp)
    for o_ref, val in zip((r_ref, lw_ref, k_ref, v_ref, kk_ref, b_ref, g_ref, bo_ref), outs):
        o_ref[0] = val
    xl_ref[0] = xn[tm - 1:tm, :]


def _prep_batch_kernel(x_ref, xp_ref, g1_ref, w_ref, mu_ref, w0_ref, wdu_ref, a0_ref, aup_ref, gup_ref,
                       kkw_ref, kaw_ref, rkw_ref,
                       r_ref, lw_ref, k_ref, v_ref, kk_ref, b_ref, g_ref, bo_ref, xn_ref):
    xn = _rmsnorm(x_ref[...], g1_ref[...])
    w = w_ref[...]
    zr = _dg(xn.astype(BF16), w, NN)
    zp = _dg(xp_ref[...].astype(BF16), w, NN)
    seg, exp = _seg_mats()
    outs = _prep_math(zr, zp, mu_ref[...], w0_ref[...], wdu_ref[...], a0_ref[...], aup_ref[...], gup_ref[...],
                      kkw_ref[...], kaw_ref[...], rkw_ref[...], seg, exp)
    for o_ref, val in zip((r_ref, lw_ref, k_ref, v_ref, kk_ref, b_ref, g_ref, bo_ref), outs):
        o_ref[...] = val
    xn_ref[...] = xn


def _prep_params(p):
    return (p['norm1_g'], p['w_rwkv'], p['mu_shift'], p['w0'], p['w_decay_up'], p['a0'], p['a_up'], p['g_up'],
            p['k_k'], p['k_a'], p['r_k'])


def _rwkv_prep_seq(x, zp0, p, tm):
    B, T, D = x.shape
    tm = min(tm, T)
    params = _prep_params(p)
    seq = pl.BlockSpec((1, tm, D), lambda b, t: (b, t, 0))
    out_shape = [jax.ShapeDtypeStruct((B, T, D), F32)] * 8 + [jax.ShapeDtypeStruct((B, 1, D), F32)]
    return pl.pallas_call(
        _prep_seq_kernel,
        grid=(B, T // tm),
        in_specs=[seq, pl.BlockSpec((1, 1, C_RWKV), lambda b, t: (b, 0, 0))] + [_full(a.shape) for a in params],
        out_specs=[seq] * 8 + [pl.BlockSpec((1, 1, D), lambda b, t: (b, 0, 0))],
        out_shape=out_shape,
        scratch_shapes=[pltpu.VMEM((1, C_RWKV), F32)],
        compiler_params=_cparams(("parallel", "arbitrary")),
        name="rwkv_prep_seq",
    )(x, zp0, *params)


def _rwkv_prep_batch(x, xprev, p):
    N, D = x.shape
    params = _prep_params(p)
    out_shape = [jax.ShapeDtypeStruct((N, D), F32)] * 9
    return pl.pallas_call(
        _prep_batch_kernel,
        grid=(1,),
        in_specs=[_full(x.shape), _full(xprev.shape)] + [_full(a.shape) for a in params],
        out_specs=[_full((N, D))] * 9,
        out_shape=out_shape,
        compiler_params=_cparams(("arbitrary",)),
        name="rwkv_prep_batch",
    )(x, xprev, *params)


HIST = 32


def _conv_tail(c, cb, lg, lb, mix_a, mix_b):
    c = c + cb
    mean = jnp.mean(c, axis=-1, keepdims=True)
    d = c - mean
    var = jnp.mean(d * d, axis=-1, keepdims=True)
    y = d * lax.rsqrt(var + LN_EPS) * lg + lb
    out_b = y * _sigmoid(y)
    return _sigmoid(mix_a), _sigmoid(mix_b) * out_b


def _glu_mix(xn, w_ref):
    xb = xn.astype(BF16)
    glu_a = _dg(xb, w_ref[:, 0:D_MODEL], NN)
    glu_b = _dg(xb, w_ref[:, D_MODEL:2 * D_MODEL], NN)
    mix_a = _dg(xb, w_ref[:, 2 * D_MODEL:3 * D_MODEL], NN)
    mix_b = _dg(xb, w_ref[:, 3 * D_MODEL:4 * D_MODEL], NN)
    return glu_a * _sigmoid(glu_b), mix_a, mix_b


def _conv_seq_kernel(x_ref, sc_ref, g1_ref, w_ref, cw_ref, cb_ref, lg_ref, lb_ref,
                     ga_ref, bm_ref, so_ref, ubuf_ref):
    t = pl.program_id(1)
    nt = pl.num_programs(1)
    tm = x_ref.shape[1]
    npast = CONV_WIDTH - 1

    @pl.when(t == 0)
    def _():
        ubuf_ref[pl.ds(HIST - npast, npast), :] = sc_ref[0]

    @pl.when(t > 0)
    def _():
        ubuf_ref[pl.ds(0, HIST), :] = ubuf_ref[pl.ds(tm, HIST), :]

    xn = _rmsnorm(x_ref[0], g1_ref[...])
    u, mix_a, mix_b = _glu_mix(xn, w_ref)
    ubuf_ref[pl.ds(HIST, tm), :] = u
    c = jnp.zeros((tm, D_MODEL), F32)
    for j in range(CONV_WIDTH):
        c = c + cw_ref[pl.ds(j, 1), :] * ubuf_ref[pl.ds(HIST - npast + j, tm), :]
    ga, bm = _conv_tail(c, cb_ref[...], lg_ref[...], lb_ref[...], mix_a, mix_b)
    ga_ref[0] = ga
    bm_ref[0] = bm

    @pl.when(t == nt - 1)
    def _():
        so_ref[0] = ubuf_ref[pl.ds(tm + HIST - npast, npast), :]


def _conv_batch_kernel(x_ref, sc_ref, g1_ref, w_ref, cw_ref, cb_ref, lg_ref, lb_ref,
                       ga_ref, bm_ref, so_ref):
    npast = CONV_WIDTH - 1
    xn = _rmsnorm(x_ref[...], g1_ref[...])
    u, mix_a, mix_b = _glu_mix(xn, w_ref)
    c = cw_ref[pl.ds(npast, 1), :] * u
    for j in range(npast):
        c = c + cw_ref[pl.ds(j, 1), :] * sc_ref[j]
    ga, bm = _conv_tail(c, cb_ref[...], lg_ref[...], lb_ref[...], mix_a, mix_b)
    ga_ref[...] = ga
    bm_ref[...] = bm
    for j in range(npast - 1):
        so_ref[j] = sc_ref[j + 1]
    so_ref[npast - 1] = u


def _conv_params(p):
    return (p['norm1_g'], p['w_rest'], p['conv_w'], p['conv_b'], p['cln_g'], p['cln_b'])


def _conv_branch_seq(x, s_conv, p, tm):
    B, T, D = x.shape
    tm = min(tm, T)
    params = _conv_params(p)
    npast = CONV_WIDTH - 1
    seq = pl.BlockSpec((1, tm, D), lambda b, t: (b, t, 0))
    st = pl.BlockSpec((1, npast, D), lambda b, t: (b, 0, 0))
    return pl.pallas_call(
        _conv_seq_kernel,
        grid=(B, T // tm),
        in_specs=[seq, st] + [_full(a.shape) for a in params],
        out_specs=[seq, seq, st],
        out_shape=[jax.ShapeDtypeStruct((B, T, D), F32)] * 2 + [jax.ShapeDtypeStruct((B, npast, D), F32)],
        scratch_shapes=[pltpu.VMEM((tm + HIST, D), F32)],
        compiler_params=_cparams(("parallel", "arbitrary")),
        name="conv_branch_seq",
    )(x, s_conv, *params)


def _conv_branch_batch(x, s_conv_t, p):
    N, D = x.shape
    params = _conv_params(p)
    return pl.pallas_call(
        _conv_batch_kernel,
        grid=(1,),
        in_specs=[_full(x.shape), _full(s_conv_t.shape)] + [_full(a.shape) for a in params],
        out_specs=[_full((N, D)), _full((N, D)), _full(s_conv_t.shape)],
        out_shape=[jax.ShapeDtypeStruct((N, D), F32)] * 2 + [jax.ShapeDtypeStruct(s_conv_t.shape, F32)],
        compiler_params=_cparams(("arbitrary",)),
        name="conv_branch_batch",
    )(x, s_conv_t, *params)


def _pair_masks(shape):
    lane = lax.broadcasted_iota(jnp.int32, shape, 1)
    return lane < HEAD_DIM


def _bd(y, m0):
    zero = jnp.zeros_like(y)
    return jnp.concatenate([jnp.where(m0, y, zero), jnp.where(m0, zero, y)], axis=0)


def _bdmm(x, y, m0, mm):
    return mm(x, _bd(y, m0), NN)


def _bdmm_nt(x, y, m0, mm):
    return mm(x, _bd(y, m0), NT)


def _bdmm_tn(x, y, m0, mm):
    a = mm(x, y, TN)
    return jnp.where(m0, a[0:HEAD_DIM, :], a[HEAD_DIM:PAIR, :])


def _chunk_pair(r, cum, lw, k, v, kk, b):
    L = CHUNK
    m0 = _pair_masks((L, PAIR))
    trow = lax.broadcasted_iota(jnp.int32, (L, PAIR), 0)
    icol = lax.broadcasted_iota(jnp.int32, (L, PAIR), 1) % HEAD_DIM
    strict = icol < trow
    incl = icol <= trow
    eye = (icol == trow).astype(F32)

    cl = cum[L - 1:L, :]
    alpha = kk * jnp.exp(cum - lw)
    rho = r * jnp.exp(cum)
    einv = jnp.exp(-cum)
    kappa = k * einv
    beta = b * einv
    etail = jnp.exp(cl - cum)
    kappa2 = k * etail
    beta2 = b * etail
    dl = jnp.exp(cl)

    zero = jnp.zeros((L, PAIR), F32)
    m_k = jnp.where(strict, _bdmm_nt(alpha, kappa, m0, _mm3), zero)
    m_b = jnp.where(strict, _bdmm_nt(alpha, beta, m0, _mm3), zero)
    n_k = jnp.where(incl, _bdmm_nt(rho, kappa, m0, _mm1), zero)
    n_b = jnp.where(incl, _bdmm_nt(rho, beta, m0, _mm1), zero)

    pw = -m_b
    tinv = eye + pw
    for _ in range(5):
        pw = _bdmm(pw, pw, m0, _mm3)
        tinv = tinv + _bdmm(tinv, pw, m0, _mm3)

    mv = _bdmm(m_k, v, m0, _mm3)
    alpha2 = _bdmm(tinv, alpha, m0, _mm3)
    w = _bdmm(tinv, mv, m0, _mm3)
    rho2 = rho - _bdmm(n_b, alpha2, m0, _mm1)
    o2 = _bdmm(n_k, v, m0, _mm1) - _bdmm(n_b, w, m0, _mm1)
    g = eye * dl - _bdmm_tn(alpha2, beta2, m0, _mm3)
    h = _bdmm_tn(v, kappa2, m0, _mm3) - _bdmm_tn(w, beta2, m0, _mm3)
    return rho2, o2, g, h


def _wkv_pre_kernel(r_ref, lw_ref, k_ref, v_ref, kk_ref, b_ref, rho_ref, o_ref, g_ref, h_ref):
    L = CHUNK
    width = r_ref.shape[2]
    ti = lax.broadcasted_iota(jnp.int32, (L, L), 0)
    tj = lax.broadcasted_iota(jnp.int32, (L, L), 1)
    ltri = (tj <= ti).astype(BF16)
    lw_all = lw_ref[0]
    h3, m3, l3 = _split3(lw_all)
    cum_all = _dg(ltri, h3, NN) + (_dg(ltri, m3, NN) + _dg(ltri, l3, NN))
    for p in range(width // PAIR):
        sl = slice(p * PAIR, (p + 1) * PAIR)
        rho2, o2, g, h = _chunk_pair(r_ref[0, :, sl], cum_all[:, sl], lw_all[:, sl], k_ref[0, :, sl],
                                     v_ref[0, :, sl], kk_ref[0, :, sl], b_ref[0, :, sl])
        rho_ref[0, :, sl] = rho2
        o_ref[0, :, sl] = o2
        g_ref[0, :, sl] = g
        h_ref[0, :, sl] = h


def _wkv_pre(r, lw, k, v, kk, b, width):
    B, T, D = r.shape
    blk = pl.BlockSpec((1, CHUNK, width), lambda bi, c, w: (bi, c, w))
    return pl.pallas_call(
        _wkv_pre_kernel,
        grid=(B, T // CHUNK, D // width),
        in_specs=[blk] * 6,
        out_specs=[blk] * 4,
        out_shape=[jax.ShapeDtypeStruct((B, T, D), F32)] * 4,
        compiler_params=_cparams(("parallel", "parallel", "parallel")),
        name="wkv_pre",
    )(r, lw, k, v, kk, b)


def _wkv_seq_kernel(rho_ref, o2_ref, g_ref, h_ref, o_ref, s_out_ref, s_ref):
    c = pl.program_id(1)
    nc = pl.num_programs(1)
    m0 = _pair_masks((CHUNK, PAIR))

    @pl.when(c == 0)
    def _():
        s_ref[...] = jnp.zeros_like(s_ref)

    for p in range(D_MODEL // PAIR):
        sl = slice(p * PAIR, (p + 1) * PAIR)
        s = s_ref[:, sl]
        o_ref[0, :, sl] = _bdmm_nt(rho_ref[0, :, sl], s, m0, _mm3) + o2_ref[0, :, sl]
        s_ref[:, sl] = _bdmm(s, g_ref[0, :, sl], m0, _mm3) + h_ref[0, :, sl]

    @pl.when(c == nc - 1)
    def _():
        for hd in range(N_HEADS):
            s_out_ref[0, hd] = s_ref[:, hd * HEAD_DIM:(hd + 1) * HEAD_DIM]


def _wkv_seq(rho2, o2, g, h):
    B, T, D = rho2.shape
    blk = pl.BlockSpec((1, CHUNK, D), lambda bi, c: (bi, c, 0))
    return pl.pallas_call(
        _wkv_seq_kernel,
        grid=(B, T // CHUNK),
        in_specs=[blk] * 4,
        out_specs=[blk, pl.BlockSpec((1, N_HEADS, HEAD_DIM, HEAD_DIM), lambda bi, c: (bi, 0, 0, 0))],
        out_shape=[jax.ShapeDtypeStruct((B, T, D), F32),
                   jax.ShapeDtypeStruct((B, N_HEADS, HEAD_DIM, HEAD_DIM), F32)],
        scratch_shapes=[pltpu.VMEM((HEAD_DIM, D), F32)],
        compiler_params=_cparams(("parallel", "arbitrary")),
        name="wkv_seq",
    )(rho2, o2, g, h)


def _wkv_step_kernel(s_ref, r_ref, lw_ref, k_ref, v_ref, kk_ref, b_ref, o_ref, so_ref):
    nb = s_ref.shape[0]
    ri = lax.broadcasted_iota(jnp.int32, (HEAD_DIM, HEAD_DIM), 0)
    ci = lax.broadcasted_iota(jnp.int32, (HEAD_DIM, HEAD_DIM), 1)
    eye = (ri == ci).astype(F32)
    for i in range(nb):
        o_parts = []
        for hd in range(N_HEADS):
            sl = slice(hd * HEAD_DIM, (hd + 1) * HEAD_DIM)
            s = s_ref[i, hd]
            row = lambda ref: ref[pl.ds(i, 1), sl]
            sa = -jnp.sum(s * row(kk_ref), axis=1, keepdims=True)
            v_col = jnp.sum(eye * row(v_ref), axis=1, keepdims=True)
            s_new = s * jnp.exp(row(lw_ref)) + sa * row(b_ref) + v_col * row(k_ref)
            so_ref[i, hd] = s_new
            o_col = jnp.sum(s_new * row(r_ref), axis=1, keepdims=True)
            o_parts.append(jnp.sum(eye * o_col, axis=0, keepdims=True))
        o_ref[pl.ds(i, 1), :] = jnp.concatenate(o_parts, axis=1)


def _wkv_step(s, r, lw, k, v, kk, b, nb):
    N = s.shape[0]
    nb = min(nb, N)
    st = pl.BlockSpec((nb, N_HEADS, HEAD_DIM, HEAD_DIM), lambda i: (i, 0, 0, 0))
    vec = pl.BlockSpec((nb, D_MODEL), lambda i: (i, 0))
    return pl.pallas_call(
        _wkv_step_kernel,
        grid=(N // nb,),
        in_specs=[st] + [vec] * 6,
        out_specs=[vec, st],
        out_shape=[jax.ShapeDtypeStruct((N, D_MODEL), F32), jax.ShapeDtypeStruct(s.shape, F32)],
        compiler_params=_cparams(("parallel",)),
        name="wkv_step",
    )(s, r, lw, k, v, kk, b)


def _first_max(x, axis, n):
    m = jnp.max(x, axis=axis, keepdims=True)
    idx = lax.broadcasted_iota(jnp.int32, x.shape, axis)
    first = jnp.min(jnp.where(x == m, idx, n), axis=axis, keepdims=True)
    return m, idx == first


def _route(scores, biased):
    tm = scores.shape[1]
    per = N_EXPERTS // N_GROUPS
    neg = jnp.full((), -jnp.inf, F32)
    b3 = biased.reshape(N_GROUPS, per, tm)
    m1, hit = _first_max(b3, 1, per)
    m2 = jnp.max(jnp.where(hit, neg, b3), axis=1, keepdims=True)
    gs = (m1 + m2).reshape(N_GROUPS, tm)
    gsel = jnp.zeros((N_GROUPS, tm), jnp.bool_)
    for _ in range(TOPK_GROUPS):
        _, hit = _first_max(gs, 0, N_GROUPS)
        gsel = jnp.logical_or(gsel, hit)
        gs = jnp.where(hit, neg, gs)
    emask = jnp.broadcast_to(gsel.reshape(N_GROUPS, 1, tm), (N_GROUPS, per, tm)).reshape(N_EXPERTS, tm)
    cand = jnp.where(emask, biased, neg)
    esel = jnp.zeros((N_EXPERTS, tm), jnp.bool_)
    for _ in range(TOP_K):
        _, hit = _first_max(cand, 0, N_EXPERTS)
        esel = jnp.logical_or(esel, hit)
        cand = jnp.where(hit, neg, cand)
    wsel = jnp.where(esel, scores, 0.0)
    return wsel / jnp.sum(wsel, axis=0, keepdims=True) * ROUTED_SCALE


def _post_kernel(o_ref, g_ref, bo_ref, ga_ref, bm_ref, x_ref,
                 gng_ref, gnb_ref, wo_ref, n2_ref, wsg_ref, wsu_ref, wsd_ref, wrt_ref, eb_ref,
                 base_ref, hn_ref, gate_ref):
    seg, exp = _seg_mats()
    o = o_ref[...]
    inv_n = 1.0 / HEAD_DIM
    mean = _mm_exact_rhs(_mm_exact_rhs(o, seg) * inv_n, exp)
    d = o - mean
    var = _mm_exact_rhs(d * d, seg) * inv_n
    rstd = _mm_exact_rhs(lax.rsqrt(var + GN_EPS), exp)
    ogn = d * rstd * gng_ref[...] + gnb_ref[...]
    out_a = (ogn + bo_ref[...]) * g_ref[...]
    merged = ga_ref[...] * out_a + bm_ref[...]
    h = x_ref[...] + _dg(merged.astype(BF16), wo_ref[...], NN)
    hn = _rmsnorm(h, n2_ref[...])
    hb = hn.astype(BF16)
    sg = _dg(hb, wsg_ref[...], NN)
    su = _dg(hb, wsu_ref[...], NN)
    shared = _dg((sg * _sigmoid(sg) * su).astype(BF16), wsd_ref[...], NN)
    base_ref[...] = h + shared
    hn_ref[...] = hb
    logits = _mm3(wrt_ref[...], hn, NT)
    scores = _sigmoid(logits)
    gate_t = _route(scores, scores + eb_ref[...])
    tm = gate_t.shape[1]
    gate_pad = jnp.concatenate([gate_t, jnp.zeros((LANES - N_EXPERTS, tm), F32)], axis=0)
    gate_ref[...] = gate_pad.T


def _post(o, g, bonus, ga, bm, x, p, tm):
    N, D = x.shape
    tm = min(tm, N)
    params = (p['gn_g'], p['gn_b'], p['w_out'], p['norm2_g'], p['ws_gate'], p['ws_up'], p['ws_down'],
              p['w_router_t'], p['e_bias'])
    row = pl.BlockSpec((tm, D), lambda i: (i, 0))
    return pl.pallas_call(
        _post_kernel,
        grid=(N // tm,),
        in_specs=[row] * 6 + [_full(a.shape) for a in params],
        out_specs=[row, row, pl.BlockSpec((tm, LANES), lambda i: (i, 0))],
        out_shape=[jax.ShapeDtypeStruct((N, D), F32), jax.ShapeDtypeStruct((N, D), BF16),
                   jax.ShapeDtypeStruct((N, LANES), F32)],
        compiler_params=_cparams(("parallel",)),
        name="post",
    )(o, g, bonus, ga, bm, x, *params)


def _moe_kernel(x_ref, gate_ref, base_ref, wg_ref, wu_ref, wd_ref, nf_ref, y_ref, acc_ref):
    e = pl.program_id(1)
    ne = pl.num_programs(1)

    @pl.when(e == 0)
    def _():
        acc_ref[...] = jnp.zeros_like(acc_ref)

    gate = gate_ref[...]
    lane = lax.broadcasted_iota(jnp.int32, gate.shape, 1)
    col = jnp.sum(jnp.where(lane == e, gate, 0.0), axis=1, keepdims=True)
    x = x_ref[...]
    hg = _dg(x, wg_ref[0], NN)
    hu = _dg(x, wu_ref[0], NN)
    hh = hg * _sigmoid(hg) * hu * col
    acc_ref[...] += _dg(hh.astype(BF16), wd_ref[0], NN)

    @pl.when(e == ne - 1)
    def _():
        y_ref[...] = _rmsnorm(base_ref[...] + acc_ref[...], nf_ref[...])


def _moe(hn, gate, base, p, tm):
    N, D = base.shape
    tm = min(tm, N)
    row = lambda w: pl.BlockSpec((tm, w), lambda i, e: (i, 0))
    return pl.pallas_call(
        _moe_kernel,
        grid=(N // tm, N_EXPERTS),
        in_specs=[row(D), row(LANES), row(D),
                  pl.BlockSpec((1, D, D_EXPERT), lambda i, e: (e, 0, 0)),
                  pl.BlockSpec((1, D, D_EXPERT), lambda i, e: (e, 0, 0)),
                  pl.BlockSpec((1, D_EXPERT, D), lambda i, e: (e, 0, 0)),
                  pl.BlockSpec((1, D), lambda i, e: (0, 0))],
        out_specs=row(D),
        out_shape=jax.ShapeDtypeStruct((N, D), F32),
        scratch_shapes=[pltpu.VMEM((tm, D), F32)],
        compiler_params=_cparams(("parallel", "arbitrary")),
        name="moe",
    )(hn, gate, base, p['w_gate'], p['w_up'], p['w_down'], p['normf_g'])


def kernel(x_prompt, x_sample, state_wkv, state_shift, state_conv, norm1_g, w_in, mu_shift, w0, w_decay_up, a0, a_up, g_up, k_k, k_a, r_k, gn_g, gn_b, conv_w, conv_b, cln_g, cln_b, w_out, norm2_g, w_router, e_bias, w_gate, w_up, w_down, ws_gate, ws_up, ws_down, normf_g):
    depth = w_in.shape[0]
    assert depth == 1
    B, T, D = x_prompt.shape
    NS = x_sample.shape[0]
    assert x_sample.shape[1] == 1 and D == D_MODEL and T % CHUNK == 0
    row = lambda a: a[0].reshape(1, -1)
    p = {
        'norm1_g': row(norm1_g), 'mu_shift': row(mu_shift), 'w0': row(w0), 'a0': row(a0),
        'k_k': row(k_k), 'k_a': row(k_a), 'r_k': row(r_k), 'gn_g': row(gn_g), 'gn_b': row(gn_b),
        'conv_b': row(conv_b), 'cln_g': row(cln_g), 'cln_b': row(cln_b), 'norm2_g': row(norm2_g),
        'normf_g': normf_g.reshape(1, -1),
        'w_rwkv': w_in[0, :, :C_RWKV].astype(BF16), 'w_rest': w_in[0, :, C_RWKV:].astype(BF16),
        'w_decay_up': w_decay_up[0], 'a_up': a_up[0], 'g_up': g_up[0], 'conv_w': conv_w[0],
        'w_out': w_out[0].astype(BF16),
        'ws_gate': ws_gate[0].astype(BF16), 'ws_up': ws_up[0].astype(BF16), 'ws_down': ws_down[0].astype(BF16),
        'w_router_t': w_router[0].T, 'e_bias': e_bias[0].reshape(-1, 1),
        'w_gate': w_gate[0].astype(BF16), 'w_up': w_up[0].astype(BF16), 'w_down': w_down[0].astype(BF16),
    }

    zp0 = jnp.zeros((B, 1, C_RWKV), F32)
    r, lw, k, v, kk, b, g, bonus, shift_p = _rwkv_prep_seq(x_prompt, zp0, p, tm=256)
    ga, bm, conv_p = _conv_branch_seq(x_prompt, jnp.zeros((B, CONV_WIDTH - 1, D), F32), p, tm=256)
    rho2, o2, gm, hm = _wkv_pre(r, lw, k, v, kk, b, width=512)
    o, wkv_p = _wkv_seq(rho2, o2, gm, hm)
    flat = lambda a: a.reshape(B * T, D)
    base, hn, gate = _post(flat(o), flat(g), flat(bonus), flat(ga), flat(bm), flat(x_prompt), p, tm=256)
    y_prompt = _moe(hn, gate, base, p, tm=1024).reshape(B, T, D)

    xs = x_sample.reshape(NS, D)
    r, lw, k, v, kk, b, g, bonus, shift_s = _rwkv_prep_batch(xs, state_shift[0], p)
    ga, bm, conv_s_t = _conv_branch_batch(xs, jnp.swapaxes(state_conv[0], 0, 1), p)
    o, wkv_s = _wkv_step(state_wkv[0], r, lw, k, v, kk, b, nb=8)
    base, hn, gate = _post(o, g, bonus, ga, bm, xs, p, tm=128)
    y_sample = _moe(hn, gate, base, p, tm=128).reshape(NS, 1, D)

    return (y_prompt, y_sample, wkv_p[None], shift_p.reshape(1, B, D), conv_p[None],
            wkv_s[None], shift_s[None], jnp.swapaxes(conv_s_t, 0, 1)[None])
```

```python
import functools

import jax
import jax.numpy as jnp
from jax import lax
from jax.experimental import pallas as pl
from jax.experimental.pallas import tpu as pltpu

F32 = jnp.float32
BF16 = jnp.bfloat16

D_MODEL = 1024
HEAD_DIM = 64
N_HEADS = D_MODEL // HEAD_DIM
D_DECAY_LORA = 64
D_AAA_LORA = 64
D_GATE_LORA = 128
GN_EPS = 64e-5
CONV_WIDTH = 31
LN_EPS = 1e-5
N_EXPERTS = 64
N_GROUPS = 8
TOPK_GROUPS = 4
TOP_K = 8
D_EXPERT = 256
ROUTED_SCALE = 2.5
RMS_EPS = 1e-6

O_K = D_MODEL
O_V = 2 * D_MODEL
O_W = 3 * D_MODEL
O_A = O_W + D_DECAY_LORA
O_G = O_A + D_AAA_LORA
C_RWKV = O_G + D_GATE_LORA
C_REST = 4 * D_MODEL

LANES = 128
CHUNK = 64
PAIR = 2 * HEAD_DIM
VMEM_LIMIT = 56 * 1024 * 1024

NN = ((1,), (0,))
NT = ((1,), (1,))
TN = ((0,), (0,))


def _dg(a, b, dims):
    return lax.dot_general(a, b, (dims, ((), ())), preferred_element_type=F32)


def _split2(x):
    hi = x.astype(BF16)
    lo = (x - hi.astype(F32)).astype(BF16)
    return hi, lo


def _split3(x):
    hi = x.astype(BF16)
    r1 = x - hi.astype(F32)
    mid = r1.astype(BF16)
    lo = (r1 - mid.astype(F32)).astype(BF16)
    return hi, mid, lo


def _mm1(a, b, dims=NN):
    return _dg(a.astype(BF16), b.astype(BF16), dims)


def _mm3(a, b, dims=NN):
    ah, al = _split2(a)
    bh, bl = _split2(b)
    return _dg(ah, bh, dims) + (_dg(ah, bl, dims) + _dg(al, bh, dims))


def _mm_exact_rhs(a, b_bf16, dims=NN):
    h, m, l = _split3(a)
    return _dg(h, b_bf16, dims) + (_dg(m, b_bf16, dims) + _dg(l, b_bf16, dims))


def _rmsnorm(x, g):
    return x * lax.rsqrt(jnp.mean(x * x, axis=-1, keepdims=True) + RMS_EPS) * g


def _sigmoid(x):
    return 1.0 / (1.0 + jnp.exp(-x))


def _seg_mats():
    row = lax.broadcasted_iota(jnp.int32, (D_MODEL, LANES), 0) // HEAD_DIM
    col = lax.broadcasted_iota(jnp.int32, (D_MODEL, LANES), 1)
    seg = (row == col).astype(BF16)
    rowt = lax.broadcasted_iota(jnp.int32, (LANES, D_MODEL), 0)
    colt = lax.broadcasted_iota(jnp.int32, (LANES, D_MODEL), 1) // HEAD_DIM
    exp = (rowt == colt).astype(BF16)
    return seg, exp


def _cparams(sem):
    return pltpu.CompilerParams(dimension_semantics=sem, vmem_limit_bytes=VMEM_LIMIT)


def _full(shape):
    n = len(shape)
    return pl.BlockSpec(shape, lambda *_: (0,) * n)


def _prep_math(zr, zp, mu, w0, wdu, a0, aup, gup, kk_w, ka_w, rk_w, seg, exp):
    zm = zr + (zp - zr) * mu
    r = zm[:, 0:O_K]
    k = zm[:, O_K:O_V]
    v = zm[:, O_V:O_W]
    xw = jnp.tanh(zm[:, O_W:O_A])
    xa = zm[:, O_A:O_G]
    xg = _sigmoid(zm[:, O_G:C_RWKV])
    y = -(w0 + _mm3(xw, wdu))
    softplus = jnp.maximum(y, 0.0) + jnp.log(1.0 + jnp.exp(-jnp.abs(y)))
    lw = -jnp.exp(-softplus - 0.5)
    a = _sigmoid(a0 + _mm3(xa, aup))
    g = _mm3(xg, gup)
    kkr = k * kk_w
    ss = _mm_exact_rhs(kkr * kkr, seg)
    inv = 1.0 / jnp.maximum(jnp.sqrt(ss), 1e-12)
    kk = kkr * _mm_exact_rhs(inv, exp)
    kf = k * (1.0 + (a - 1.0) * ka_w)
    b = kk * a
    rk = _mm_exact_rhs(r * kf * rk_w, seg)
    bonus = _mm_exact_rhs(rk, exp) * v
    return r, lw, kf, v, kk, b, g, bonus


def _prep_seq_kernel(x_ref, zp0_ref, g1_ref, w_ref, mu_ref, w0_ref, wdu_ref, a0_ref, aup_ref, gup_ref,
                     kkw_ref, kaw_ref, rkw_ref,
                     r_ref, lw_ref, k_ref, v_ref, kk_ref, b_ref, g_ref, bo_ref, xl_ref, carry_ref):
    t = pl.program_id(1)
    tm = x_ref.shape[1]

    @pl.when(t == 0)
    def _():
        carry_ref[...] = zp0_ref[0]

    xn = _rmsnorm(x_ref[0], g1_ref[...])
    zr = _dg(xn.astype(BF16), w_ref[...], NN)
    rows = lax.broadcasted_iota(jnp.int32, zr.shape, 0)
    zp = jnp.where(rows == 0, carry_ref[...], pltpu.roll(zr, 1, 0))
    carry_ref[...] = zr[tm - 1:tm, :]
    seg, exp = _seg_mats()
    outs = _prep_math(zr, zp, mu_ref[...], w0_ref[...], wdu_ref[...], a0_ref[...], aup_ref[...], gup_ref[...],
                      kkw_ref[...], kaw_ref[...], rkw_ref[...], seg, exp)
    for o_ref, val in zip((r_ref, lw_ref, k_ref, v_ref, kk_ref, b_ref, g_ref, bo_ref), outs):
        o_ref[0] = val
    xl_ref[0] = xn[tm - 1:tm, :]


def _prep_batch_kernel(x_ref, xp_ref, g1_ref, w_ref, mu_ref, w0_ref, wdu_ref, a0_ref, aup_ref, gup_ref,
                       kkw_ref, kaw_ref, rkw_ref,
                       r_ref, lw_ref, k_ref, v_ref, kk_ref, b_ref, g_ref, bo_ref, xn_ref):
    xn = _rmsnorm(x_ref[...], g1_ref[...])
    w = w_ref[...]
    zr = _dg(xn.astype(BF16), w, NN)
    zp = _dg(xp_ref[...].astype(BF16), w, NN)
    seg, exp = _seg_mats()
    outs = _prep_math(zr, zp, mu_ref[...], w0_ref[...], wdu_ref[...], a0_ref[...], aup_ref[...], gup_ref[...],
                      kkw_ref[...], kaw_ref[...], rkw_ref[...], seg, exp)
    for o_ref, val in zip((r_ref, lw_ref, k_ref, v_ref, kk_ref, b_ref, g_ref, bo_ref), outs):
        o_ref[...] = val
    xn_ref[...] = xn


def _prep_params(p):
    return (p['norm1_g'], p['w_rwkv'], p['mu_shift'], p['w0'], p['w_decay_up'], p['a0'], p['a_up'], p['g_up'],
            p['k_k'], p['k_a'], p['r_k'])


def _rwkv_prep_seq(x, zp0, p, tm):
    B, T, D = x.shape
    tm = min(tm, T)
    params = _prep_params(p)
    seq = pl.BlockSpec((1, tm, D), lambda b, t: (b, t, 0))
    out_shape = [jax.ShapeDtypeStruct((B, T, D), F32)] * 8 + [jax.ShapeDtypeStruct((B, 1, D), F32)]
    return pl.pallas_call(
        _prep_seq_kernel,
        grid=(B, T // tm),
        in_specs=[seq, pl.BlockSpec((1, 1, C_RWKV), lambda b, t: (b, 0, 0))] + [_full(a.shape) for a in params],
        out_specs=[seq] * 8 + [pl.BlockSpec((1, 1, D), lambda b, t: (b, 0, 0))],
        out_shape=out_shape,
        scratch_shapes=[pltpu.VMEM((1, C_RWKV), F32)],
        compiler_params=_cparams(("parallel", "arbitrary")),
        name="rwkv_prep_seq",
    )(x, zp0, *params)


def _rwkv_prep_batch(x, xprev, p):
    N, D = x.shape
    params = _prep_params(p)
    out_shape = [jax.ShapeDtypeStruct((N, D), F32)] * 9
    return pl.pallas_call(
        _prep_batch_kernel,
        grid=(1,),
        in_specs=[_full(x.shape), _full(xprev.shape)] + [_full(a.shape) for a in params],
        out_specs=[_full((N, D))] * 9,
        out_shape=out_shape,
        compiler_params=_cparams(("arbitrary",)),
        name="rwkv_prep_batch",
    )(x, xprev, *params)


HIST = 32


def _conv_tail(c, cb, lg, lb, mix_a, mix_b):
    c = c + cb
    mean = jnp.mean(c, axis=-1, keepdims=True)
    d = c - mean
    var = jnp.mean(d * d, axis=-1, keepdims=True)
    y = d * lax.rsqrt(var + LN_EPS) * lg + lb
    out_b = y * _sigmoid(y)
    return _sigmoid(mix_a), _sigmoid(mix_b) * out_b


def _glu_mix(xn, w_ref):
    xb = xn.astype(BF16)
    glu_a = _dg(xb, w_ref[:, 0:D_MODEL], NN)
    glu_b = _dg(xb, w_ref[:, D_MODEL:2 * D_MODEL], NN)
    mix_a = _dg(xb, w_ref[:, 2 * D_MODEL:3 * D_MODEL], NN)
    mix_b = _dg(xb, w_ref[:, 3 * D_MODEL:4 * D_MODEL], NN)
    return glu_a * _sigmoid(glu_b), mix_a, mix_b


def _conv_seq_kernel(x_ref, sc_ref, g1_ref, w_ref, cw_ref, cb_ref, lg_ref, lb_ref,
                     ga_ref, bm_ref, so_ref, ubuf_ref):
    t = pl.program_id(1)
    nt = pl.num_programs(1)
    tm = x_ref.shape[1]
    npast = CONV_WIDTH - 1

    @pl.when(t == 0)
    def _():
        ubuf_ref[pl.ds(HIST - npast, npast), :] = sc_ref[0]

    @pl.when(t > 0)
    def _():
        ubuf_ref[pl.ds(0, HIST), :] = ubuf_ref[pl.ds(tm, HIST), :]

    xn = _rmsnorm(x_ref[0], g1_ref[...])
    u, mix_a, mix_b = _glu_mix(xn, w_ref)
    ubuf_ref[pl.ds(HIST, tm), :] = u
    c = jnp.zeros((tm, D_MODEL), F32)
    for j in range(CONV_WIDTH):
        c = c + cw_ref[pl.ds(j, 1), :] * ubuf_ref[pl.ds(HIST - npast + j, tm), :]
    ga, bm = _conv_tail(c, cb_ref[...], lg_ref[...], lb_ref[...], mix_a, mix_b)
    ga_ref[0] = ga
    bm_ref[0] = bm

    @pl.when(t == nt - 1)
    def _():
        so_ref[0] = ubuf_ref[pl.ds(tm + HIST - npast, npast), :]


def _conv_batch_kernel(x_ref, sc_ref, g1_ref, w_ref, cw_ref, cb_ref, lg_ref, lb_ref,
                       ga_ref, bm_ref, so_ref):
    npast = CONV_WIDTH - 1
    xn = _rmsnorm(x_ref[...], g1_ref[...])
    u, mix_a, mix_b = _glu_mix(xn, w_ref)
    c = cw_ref[pl.ds(npast, 1), :] * u
    for j in range(npast):
        c = c + cw_ref[pl.ds(j, 1), :] * sc_ref[j]
    ga, bm = _conv_tail(c, cb_ref[...], lg_ref[...], lb_ref[...], mix_a, mix_b)
    ga_ref[...] = ga
    bm_ref[...] = bm
    for j in range(npast - 1):
        so_ref[j] = sc_ref[j + 1]
    so_ref[npast - 1] = u


def _conv_params(p):
    return (p['norm1_g'], p['w_rest'], p['conv_w'], p['conv_b'], p['cln_g'], p['cln_b'])


def _conv_branch_seq(x, s_conv, p, tm):
    B, T, D = x.shape
    tm = min(tm, T)
    params = _conv_params(p)
    npast = CONV_WIDTH - 1
    seq = pl.BlockSpec((1, tm, D), lambda b, t: (b, t, 0))
    st = pl.BlockSpec((1, npast, D), lambda b, t: (b, 0, 0))
    return pl.pallas_call(
        _conv_seq_kernel,
        grid=(B, T // tm),
        in_specs=[seq, st] + [_full(a.shape) for a in params],
        out_specs=[seq, seq, st],
        out_shape=[jax.ShapeDtypeStruct((B, T, D), F32)] * 2 + [jax.ShapeDtypeStruct((B, npast, D), F32)],
        scratch_shapes=[pltpu.VMEM((tm + HIST, D), F32)],
        compiler_params=_cparams(("parallel", "arbitrary")),
        name="conv_branch_seq",
    )(x, s_conv, *params)


def _conv_branch_batch(x, s_conv_t, p):
    N, D = x.shape
    params = _conv_params(p)
    return pl.pallas_call(
        _conv_batch_kernel,
        grid=(1,),
        in_specs=[_full(x.shape), _full(s_conv_t.shape)] + [_full(a.shape) for a in params],
        out_specs=[_full((N, D)), _full((N, D)), _full(s_conv_t.shape)],
        out_shape=[jax.ShapeDtypeStruct((N, D), F32)] * 2 + [jax.ShapeDtypeStruct(s_conv_t.shape, F32)],
        compiler_params=_cparams(("arbitrary",)),
        name="conv_branch_batch",
    )(x, s_conv_t, *params)


def _pair_masks(shape):
    lane = lax.broadcasted_iota(jnp.int32, shape, 1)
    return lane < HEAD_DIM


def _bd(y, m0):
    zero = jnp.zeros_like(y)
    return jnp.concatenate([jnp.where(m0, y, zero), jnp.where(m0, zero, y)], axis=0)


def _bdmm(x, y, m0, mm):
    return mm(x, _bd(y, m0), NN)


def _bdmm_nt(x, y, m0, mm):
    return mm(x, _bd(y, m0), NT)


def _bdmm_tn(x, y, m0, mm):
    a = mm(x, y, TN)
    return jnp.where(m0, a[0:HEAD_DIM, :], a[HEAD_DIM:PAIR, :])


def _map(f, *lists):
    return [f(*xs) for xs in zip(*lists)]


def _chunk_pairs(r, cum, lw, k, v, kk, b):
    L = CHUNK
    m0 = _pair_masks((L, PAIR))
    trow = lax.broadcasted_iota(jnp.int32, (L, PAIR), 0)
    icol = lax.broadcasted_iota(jnp.int32, (L, PAIR), 1) % HEAD_DIM
    strict = icol < trow
    incl = icol <= trow
    eye = (icol == trow).astype(F32)
    zero = jnp.zeros((L, PAIR), F32)

    cl = [c[L - 1:L, :] for c in cum]
    alpha = _map(lambda x, c, w: x * jnp.exp(c - w), kk, cum, lw)
    rho = _map(lambda x, c: x * jnp.exp(c), r, cum)
    einv = [jnp.exp(-c) for c in cum]
    kappa = _map(lambda x, e: x * e, k, einv)
    beta = _map(lambda x, e: x * e, b, einv)
    etail = _map(lambda c1, c: jnp.exp(c1 - c), cl, cum)
    kappa2 = _map(lambda x, e: x * e, k, etail)
    beta2 = _map(lambda x, e: x * e, b, etail)
    dl = [jnp.exp(c1) for c1 in cl]

    mm = _mm1
    m_b = _map(lambda x, y: jnp.where(strict, _bdmm_nt(x, y, m0, mm), zero), alpha, beta)
    m_k = _map(lambda x, y: jnp.where(strict, _bdmm_nt(x, y, m0, mm), zero), alpha, kappa)
    n_k = _map(lambda x, y: jnp.where(incl, _bdmm_nt(x, y, m0, mm), zero), rho, kappa)
    n_b = _map(lambda x, y: jnp.where(incl, _bdmm_nt(x, y, m0, mm), zero), rho, beta)

    pw = [-m for m in m_b]
    tinv = [eye + q for q in pw]
    mv = _map(lambda x, y: _bdmm(x, y, m0, mm), m_k, v)
    for _ in range(5):
        pw = _map(lambda q: _bdmm(q, q, m0, mm), pw)
        tinv = _map(lambda t, q: t + _bdmm(t, q, m0, mm), tinv, pw)

    alpha2 = _map(lambda t, x: _bdmm(t, x, m0, mm), tinv, alpha)
    w = _map(lambda t, x: _bdmm(t, x, m0, mm), tinv, mv)
    nkv = _map(lambda x, y: _bdmm(x, y, m0, mm), n_k, v)
    rho2 = _map(lambda x, n, a2: x - _bdmm(n, a2, m0, mm), rho, n_b, alpha2)
    o2 = _map(lambda x, n, y: x - _bdmm(n, y, m0, mm), nkv, n_b, w)
    g = _map(lambda d, a2, b2: eye * d - _bdmm_tn(a2, b2, m0, mm), dl, alpha2, beta2)
    h = _map(lambda x, k2, y, b2: _bdmm_tn(x, k2, m0, mm) - _bdmm_tn(y, b2, m0, mm), v, kappa2, w, beta2)
    return rho2, o2, g, h


def _wkv_pre_kernel(r_ref, lw_ref, k_ref, v_ref, kk_ref, b_ref, rho_ref, o_ref, g_ref, h_ref):
    L = CHUNK
    width = r_ref.shape[2]
    ti = lax.broadcasted_iota(jnp.int32, (L, L), 0)
    tj = lax.broadcasted_iota(jnp.int32, (L, L), 1)
    ltri = (tj <= ti).astype(BF16)
    lw_all = lw_ref[0]
    h3, m3, l3 = _split3(lw_all)
    cum_all = _dg(ltri, h3, NN) + (_dg(ltri, m3, NN) + _dg(ltri, l3, NN))
    sls = [slice(p * PAIR, (p + 1) * PAIR) for p in range(width // PAIR)]
    pick = lambda ref: [ref[0, :, sl] for sl in sls]
    outs = _chunk_pairs(pick(r_ref), [cum_all[:, sl] for sl in sls], [lw_all[:, sl] for sl in sls],
                        pick(k_ref), pick(v_ref), pick(kk_ref), pick(b_ref))
    for ref, vals in zip((rho_ref, o_ref, g_ref, h_ref), outs):
        for sl, val in zip(sls, vals):
            ref[0, :, sl] = val


def _wkv_pre(r, lw, k, v, kk, b, width):
    B, T, D = r.shape
    blk = pl.BlockSpec((1, CHUNK, width), lambda bi, c, w: (bi, c, w))
    return pl.pallas_call(
        _wkv_pre_kernel,
        grid=(B, T // CHUNK, D // width),
        in_specs=[blk] * 6,
        out_specs=[blk] * 4,
        out_shape=[jax.ShapeDtypeStruct((B, T, D), F32)] * 4,
        compiler_params=_cparams(("parallel", "parallel", "parallel")),
        name="wkv_pre",
    )(r, lw, k, v, kk, b)


def _wkv_seq_kernel(rho_ref, o2_ref, g_ref, h_ref, o_ref, s_out_ref, s_ref):
    c = pl.program_id(1)
    nc = pl.num_programs(1)
    m0 = _pair_masks((CHUNK, PAIR))

    @pl.when(c == 0)
    def _():
        s_ref[...] = jnp.zeros_like(s_ref)

    sls = [slice(p * PAIR, (p + 1) * PAIR) for p in range(D_MODEL // PAIR)]
    s = [s_ref[:, sl] for sl in sls]
    s_new = [_bdmm(x, g_ref[0, :, sl], m0, _mm3) + h_ref[0, :, sl] for sl, x in zip(sls, s)]
    o = [_bdmm_nt(rho_ref[0, :, sl], x, m0, _mm3) + o2_ref[0, :, sl] for sl, x in zip(sls, s)]
    for sl, ov, sv in zip(sls, o, s_new):
        s_ref[:, sl] = sv
        o_ref[0, :, sl] = ov

    @pl.when(c == nc - 1)
    def _():
        for hd in range(N_HEADS):
            s_out_ref[0, hd] = s_ref[:, hd * HEAD_DIM:(hd + 1) * HEAD_DIM]


def _wkv_seq(rho2, o2, g, h):
    B, T, D = rho2.shape
    blk = pl.BlockSpec((1, CHUNK, D), lambda bi, c: (bi, c, 0))
    return pl.pallas_call(
        _wkv_seq_kernel,
        grid=(B, T // CHUNK),
        in_specs=[blk] * 4,
        out_specs=[blk, pl.BlockSpec((1, N_HEADS, HEAD_DIM, HEAD_DIM), lambda bi, c: (bi, 0, 0, 0))],
        out_shape=[jax.ShapeDtypeStruct((B, T, D), F32),
                   jax.ShapeDtypeStruct((B, N_HEADS, HEAD_DIM, HEAD_DIM), F32)],
        scratch_shapes=[pltpu.VMEM((HEAD_DIM, D), F32)],
        compiler_params=_cparams(("parallel", "arbitrary")),
        name="wkv_seq",
    )(rho2, o2, g, h)


def _wkv_step_kernel(s_ref, r_ref, lw_ref, k_ref, v_ref, kk_ref, b_ref, o_ref, so_ref):
    nb = s_ref.shape[0]
    npair = N_HEADS // 2
    lane = lax.broadcasted_iota(jnp.int32, (HEAD_DIM, PAIR), 1)
    rowi = lax.broadcasted_iota(jnp.int32, (HEAD_DIM, PAIR), 0)
    eyep = (lane % HEAD_DIM == rowi).astype(F32)
    bi = lax.broadcasted_iota(jnp.int32, (PAIR, PAIR), 0) // HEAD_DIM
    bj = lax.broadcasted_iota(jnp.int32, (PAIR, PAIR), 1) // HEAD_DIM
    ones2 = (bi == bj).astype(BF16)
    items = [(i, p) for i in range(nb) for p in range(npair)]
    n = len(items)
    row = lambda ref, i, p: ref[pl.ds(i, 1), p * PAIR:(p + 1) * PAIR]
    sp = [jnp.concatenate([s_ref[i, 2 * p], s_ref[i, 2 * p + 1]], axis=1) for i, p in items]
    stack = jnp.concatenate([s * row(kk_ref, i, p) for s, (i, p) in zip(sp, items)]
                            + [eyep * row(v_ref, i, p) for i, p in items], axis=0)
    red = _mm_exact_rhs(stack, ones2)
    blk = lambda a, j: a[j * HEAD_DIM:(j + 1) * HEAD_DIM, :]
    s_new = [s * jnp.exp(row(lw_ref, i, p)) - blk(red, j) * row(b_ref, i, p) + blk(red, n + j) * row(k_ref, i, p)
             for j, (s, (i, p)) in enumerate(zip(sp, items))]
    for s, (i, p) in zip(s_new, items):
        so_ref[i, 2 * p] = s[:, :HEAD_DIM]
        so_ref[i, 2 * p + 1] = s[:, HEAD_DIM:]
    ored = _mm_exact_rhs(jnp.concatenate([s * row(r_ref, i, p) for s, (i, p) in zip(s_new, items)], axis=0), ones2)
    o_row = [jnp.sum(eyep * blk(ored, j), axis=0, keepdims=True) for j in range(n)]
    for i in range(nb):
        o_ref[pl.ds(i, 1), :] = jnp.concatenate(o_row[i * npair:(i + 1) * npair], axis=1)


def _wkv_step(s, r, lw, k, v, kk, b, nb):
    N = s.shape[0]
    nb = min(nb, N)
    st = pl.BlockSpec((nb, N_HEADS, HEAD_DIM, HEAD_DIM), lambda i: (i, 0, 0, 0))
    vec = pl.BlockSpec((nb, D_MODEL), lambda i: (i, 0))
    return pl.pallas_call(
        _wkv_step_kernel,
        grid=(N // nb,),
        in_specs=[st] + [vec] * 6,
        out_specs=[vec, st],
        out_shape=[jax.ShapeDtypeStruct((N, D_MODEL), F32), jax.ShapeDtypeStruct(s.shape, F32)],
        compiler_params=_cparams(("parallel",)),
        name="wkv_step",
    )(s, r, lw, k, v, kk, b)


def _first_max(x, axis, n):
    m = jnp.max(x, axis=axis, keepdims=True)
    idx = lax.broadcasted_iota(jnp.int32, x.shape, axis)
    first = jnp.min(jnp.where(x == m, idx, n), axis=axis, keepdims=True)
    return m, idx == first


def _route(scores, biased):
    tm = scores.shape[1]
    per = N_EXPERTS // N_GROUPS
    neg = jnp.full((), -jnp.inf, F32)
    b3 = biased.reshape(N_GROUPS, per, tm)
    m1, hit = _first_max(b3, 1, per)
    m2 = jnp.max(jnp.where(hit, neg, b3), axis=1, keepdims=True)
    gs = (m1 + m2).reshape(N_GROUPS, tm)
    gsel = jnp.zeros((N_GROUPS, tm), jnp.bool_)
    for _ in range(TOPK_GROUPS):
        _, hit = _first_max(gs, 0, N_GROUPS)
        gsel = jnp.logical_or(gsel, hit)
        gs = jnp.where(hit, neg, gs)
    emask = jnp.broadcast_to(gsel.reshape(N_GROUPS, 1, tm), (N_GROUPS, per, tm)).reshape(N_EXPERTS, tm)
    cand = jnp.where(emask, biased, neg)
    esel = jnp.zeros((N_EXPERTS, tm), jnp.bool_)
    for _ in range(TOP_K):
        _, hit = _first_max(cand, 0, N_EXPERTS)
        esel = jnp.logical_or(esel, hit)
        cand = jnp.where(hit, neg, cand)
    wsel = jnp.where(esel, scores, 0.0)
    return wsel / jnp.sum(wsel, axis=0, keepdims=True) * ROUTED_SCALE


def _post_kernel(o_ref, g_ref, bo_ref, ga_ref, bm_ref, x_ref,
                 gng_ref, gnb_ref, wo_ref, n2_ref, wsg_ref, wsu_ref, wsd_ref, wrt_ref, eb_ref,
                 base_ref, hn_ref, gate_ref):
    seg, exp = _seg_mats()
    o = o_ref[...]
    inv_n = 1.0 / HEAD_DIM
    mean = _mm_exact_rhs(_mm_exact_rhs(o, seg) * inv_n, exp)
    d = o - mean
    var = _mm_exact_rhs(d * d, seg) * inv_n
    rstd = _mm_exact_rhs(lax.rsqrt(var + GN_EPS), exp)
    ogn = d * rstd * gng_ref[...] + gnb_ref[...]
    out_a = (ogn + bo_ref[...]) * g_ref[...]
    merged = ga_ref[...] * out_a + bm_ref[...]
    h = x_ref[...] + _dg(merged.astype(BF16), wo_ref[...], NN)
    hn = _rmsnorm(h, n2_ref[...])
    hb = hn.astype(BF16)
    sg = _dg(hb, wsg_ref[...], NN)
    su = _dg(hb, wsu_ref[...], NN)
    shared = _dg((sg * _sigmoid(sg) * su).astype(BF16), wsd_ref[...], NN)
    base_ref[...] = h + shared
    hn_ref[...] = hb
    logits = _mm3(wrt_ref[...], hn, NT)
    scores = _sigmoid(logits)
    gate_t = _route(scores, scores + eb_ref[...])
    tm = gate_t.shape[1]
    gate_pad = jnp.concatenate([gate_t, jnp.zeros((LANES - N_EXPERTS, tm), F32)], axis=0)
    gate_ref[...] = gate_pad.T


def _post(o, g, bonus, ga, bm, x, p, tm):
    N, D = x.shape
    tm = min(tm, N)
    params = (p['gn_g'], p['gn_b'], p['w_out'], p['norm2_g'], p['ws_gate'], p['ws_up'], p['ws_down'],
              p['w_router_t'], p['e_bias'])
    row = pl.BlockSpec((tm, D), lambda i: (i, 0))
    return pl.pallas_call(
        _post_kernel,
        grid=(N // tm,),
        in_specs=[row] * 6 + [_full(a.shape) for a in params],
        out_specs=[row, row, pl.BlockSpec((tm, LANES), lambda i: (i, 0))],
        out_shape=[jax.ShapeDtypeStruct((N, D), F32), jax.ShapeDtypeStruct((N, D), BF16),
                   jax.ShapeDtypeStruct((N, LANES), F32)],
        compiler_params=_cparams(("parallel",)),
        name="post",
    )(o, g, bonus, ga, bm, x, *params)


def _moe_kernel(x_ref, gate_ref, base_ref, wg_ref, wu_ref, wd_ref, nf_ref, y_ref):
    j = pl.program_id(1)
    nj = pl.num_programs(1)
    eps = wg_ref.shape[0]
    gate = gate_ref[...]
    lane = lax.broadcasted_iota(jnp.int32, gate.shape, 1)
    x = x_ref[...]
    cols = [jnp.sum(jnp.where(lane == j * eps + q, gate, 0.0), axis=1, keepdims=True) for q in range(eps)]
    hg = [_dg(x, wg_ref[q], NN) for q in range(eps)]
    hu = [_dg(x, wu_ref[q], NN) for q in range(eps)]
    hh = [(hg[q] * _sigmoid(hg[q]) * hu[q] * cols[q]).astype(BF16) for q in range(eps)]
    down = _dg(jnp.concatenate(hh, axis=1), wd_ref[...].reshape(eps * D_EXPERT, D_MODEL), NN)

    @pl.when(j == 0)
    def _():
        y_ref[...] = base_ref[...] + down

    @pl.when(j > 0)
    def _():
        y_ref[...] += down

    @pl.when(j == nj - 1)
    def _():
        y_ref[...] = _rmsnorm(y_ref[...], nf_ref[...])


def _moe(hn, gate, base, p, tm, eps):
    N, D = base.shape
    tm = min(tm, N)
    row = lambda w: pl.BlockSpec((tm, w), lambda i, e: (i, 0))
    return pl.pallas_call(
        _moe_kernel,
        grid=(N // tm, N_EXPERTS // eps),
        in_specs=[row(D), row(LANES), row(D),
                  pl.BlockSpec((eps, D, D_EXPERT), lambda i, e: (e, 0, 0)),
                  pl.BlockSpec((eps, D, D_EXPERT), lambda i, e: (e, 0, 0)),
                  pl.BlockSpec((eps, D_EXPERT, D), lambda i, e: (e, 0, 0)),
                  pl.BlockSpec((1, D), lambda i, e: (0, 0))],
        out_specs=row(D),
        out_shape=jax.ShapeDtypeStruct((N, D), F32),
        compiler_params=_cparams(("parallel", "arbitrary")),
        name="moe",
    )(hn, gate, base, p['w_gate'], p['w_up'], p['w_down'], p['normf_g'])


def kernel(x_prompt, x_sample, state_wkv, state_shift, state_conv, norm1_g, w_in, mu_shift, w0, w_decay_up, a0, a_up, g_up, k_k, k_a, r_k, gn_g, gn_b, conv_w, conv_b, cln_g, cln_b, w_out, norm2_g, w_router, e_bias, w_gate, w_up, w_down, ws_gate, ws_up, ws_down, normf_g):
    depth = w_in.shape[0]
    assert depth == 1
    B, T, D = x_prompt.shape
    NS = x_sample.shape[0]
    assert x_sample.shape[1] == 1 and D == D_MODEL and T % CHUNK == 0
    row = lambda a: a[0].reshape(1, -1)
    p = {
        'norm1_g': row(norm1_g), 'mu_shift': row(mu_shift), 'w0': row(w0), 'a0': row(a0),
        'k_k': row(k_k), 'k_a': row(k_a), 'r_k': row(r_k), 'gn_g': row(gn_g), 'gn_b': row(gn_b),
        'conv_b': row(conv_b), 'cln_g': row(cln_g), 'cln_b': row(cln_b), 'norm2_g': row(norm2_g),
        'normf_g': normf_g.reshape(1, -1),
        'w_rwkv': w_in[0, :, :C_RWKV].astype(BF16), 'w_rest': w_in[0, :, C_RWKV:].astype(BF16),
        'w_decay_up': w_decay_up[0], 'a_up': a_up[0], 'g_up': g_up[0], 'conv_w': conv_w[0],
        'w_out': w_out[0].astype(BF16),
        'ws_gate': ws_gate[0].astype(BF16), 'ws_up': ws_up[0].astype(BF16), 'ws_down': ws_down[0].astype(BF16),
        'w_router_t': w_router[0].T, 'e_bias': e_bias[0].reshape(-1, 1),
        'w_gate': w_gate[0].astype(BF16), 'w_up': w_up[0].astype(BF16), 'w_down': w_down[0].astype(BF16),
    }

    zp0 = jnp.zeros((B, 1, C_RWKV), F32)
    r, lw, k, v, kk, b, g, bonus, shift_p = _rwkv_prep_seq(x_prompt, zp0, p, tm=256)
    ga, bm, conv_p = _conv_branch_seq(x_prompt, jnp.zeros((B, CONV_WIDTH - 1, D), F32), p, tm=256)
    rho2, o2, gm, hm = _wkv_pre(r, lw, k, v, kk, b, width=D_MODEL)
    o, wkv_p = _wkv_seq(rho2, o2, gm, hm)
    flat = lambda a: a.reshape(B * T, D)
    base, hn, gate = _post(flat(o), flat(g), flat(bonus), flat(ga), flat(bm), flat(x_prompt), p, tm=256)
    y_prompt = _moe(hn, gate, base, p, tm=1024, eps=4).reshape(B, T, D)

    xs = x_sample.reshape(NS, D)
    r, lw, k, v, kk, b, g, bonus, shift_s = _rwkv_prep_batch(xs, state_shift[0], p)
    ga, bm, conv_s_t = _conv_branch_batch(xs, jnp.swapaxes(state_conv[0], 0, 1), p)
    o, wkv_s = _wkv_step(state_wkv[0], r, lw, k, v, kk, b, nb=8)
    base, hn, gate = _post(o, g, bonus, ga, bm, xs, p, tm=128)
    y_sample = _moe(hn, gate, base, p, tm=128, eps=4).reshape(NS, 1, D)

    return (y_prompt, y_sample, wkv_p[None], shift_p.reshape(1, B, D), conv_p[None],
            wkv_s[None], shift_s[None], jnp.swapaxes(conv_s_t, 0, 1)[None])
```

```python
import functools

import jax
import jax.numpy as jnp
from jax import lax
from jax.experimental import pallas as pl
from jax.experimental.pallas import tpu as pltpu

F32 = jnp.float32
BF16 = jnp.bfloat16

D_MODEL = 1024
HEAD_DIM = 64
N_HEADS = D_MODEL // HEAD_DIM
D_DECAY_LORA = 64
D_AAA_LORA = 64
D_GATE_LORA = 128
GN_EPS = 64e-5
CONV_WIDTH = 31
LN_EPS = 1e-5
N_EXPERTS = 64
N_GROUPS = 8
TOPK_GROUPS = 4
TOP_K = 8
D_EXPERT = 256
ROUTED_SCALE = 2.5
RMS_EPS = 1e-6

O_K = D_MODEL
O_V = 2 * D_MODEL
O_W = 3 * D_MODEL
O_A = O_W + D_DECAY_LORA
O_G = O_A + D_AAA_LORA
C_RWKV = O_G + D_GATE_LORA
C_REST = 4 * D_MODEL

LANES = 128
SUBLANES = 8
CHUNK = 64
PAIR = 2 * HEAD_DIM
VMEM_LIMIT = 56 * 1024 * 1024

NN = ((1,), (0,))
NT = ((1,), (1,))
TN = ((0,), (0,))


def _dg(a, b, dims):
    return lax.dot_general(a, b, (dims, ((), ())), preferred_element_type=F32)


def _split2(x):
    hi = x.astype(BF16)
    lo = (x - hi.astype(F32)).astype(BF16)
    return hi, lo


def _split3(x):
    hi = x.astype(BF16)
    r1 = x - hi.astype(F32)
    mid = r1.astype(BF16)
    lo = (r1 - mid.astype(F32)).astype(BF16)
    return hi, mid, lo


def _mm1(a, b, dims=NN):
    return _dg(a.astype(BF16), b.astype(BF16), dims)


def _mm3(a, b, dims=NN):
    ah, al = _split2(a)
    bh, bl = _split2(b)
    return _dg(ah, bh, dims) + (_dg(ah, bl, dims) + _dg(al, bh, dims))


def _mm_exact_rhs(a, b_bf16, dims=NN, passes=3):
    if passes == 1:
        return _dg(a.astype(BF16), b_bf16, dims)
    if passes == 2:
        h, l = _split2(a)
        return _dg(h, b_bf16, dims) + _dg(l, b_bf16, dims)
    h, m, l = _split3(a)
    return _dg(h, b_bf16, dims) + (_dg(m, b_bf16, dims) + _dg(l, b_bf16, dims))


def _rmsnorm(x, g):
    return x * lax.rsqrt(jnp.mean(x * x, axis=-1, keepdims=True) + RMS_EPS) * g


def _sigmoid(x):
    return 1.0 / (1.0 + jnp.exp(-x))


def _seg_mats():
    row = lax.broadcasted_iota(jnp.int32, (D_MODEL, LANES), 0) // HEAD_DIM
    col = lax.broadcasted_iota(jnp.int32, (D_MODEL, LANES), 1)
    seg = (row == col).astype(BF16)
    rowt = lax.broadcasted_iota(jnp.int32, (LANES, D_MODEL), 0)
    colt = lax.broadcasted_iota(jnp.int32, (LANES, D_MODEL), 1) // HEAD_DIM
    exp = (rowt == colt).astype(BF16)
    return seg, exp


def _cparams(sem):
    return pltpu.CompilerParams(dimension_semantics=sem, vmem_limit_bytes=VMEM_LIMIT)


def _full(shape):
    n = len(shape)
    return pl.BlockSpec(shape, lambda *_: (0,) * n)


def _prep_math(zr, zp, mu, w0, wdu, a0, aup, gup, kk_w, ka_w, rk_w, seg, exp):
    zm = zr + (zp - zr) * mu
    r = zm[:, 0:O_K]
    k = zm[:, O_K:O_V]
    v = zm[:, O_V:O_W]
    xw = jnp.tanh(zm[:, O_W:O_A])
    xa = zm[:, O_A:O_G]
    xg = _sigmoid(zm[:, O_G:C_RWKV])
    y = -(w0 + _mm1(xw, wdu))
    softplus = jnp.maximum(y, 0.0) + jnp.log(1.0 + jnp.exp(-jnp.abs(y)))
    lw = -jnp.exp(-softplus - 0.5)
    a = _sigmoid(a0 + _mm1(xa, aup))
    g = _mm1(xg, gup)
    kkr = k * kk_w
    ss = _mm_exact_rhs(kkr * kkr, seg, passes=2)
    inv = 1.0 / jnp.maximum(jnp.sqrt(ss), 1e-12)
    kk = kkr * _mm_exact_rhs(inv, exp, passes=2)
    kf = k * (1.0 + (a - 1.0) * ka_w)
    b = kk * a
    rk = _mm_exact_rhs(r * kf * rk_w, seg, passes=1)
    bonus = _mm_exact_rhs(rk, exp, passes=1) * v
    return r, lw, kf, v, kk, b, g, bonus


def _prep_seq_kernel(x_ref, zp0_ref, g1_ref, w_ref, mu_ref, w0_ref, wdu_ref, a0_ref, aup_ref, gup_ref,
                     kkw_ref, kaw_ref, rkw_ref,
                     r_ref, lw_ref, k_ref, v_ref, kk_ref, b_ref, g_ref, bo_ref, xl_ref, carry_ref):
    t = pl.program_id(1)
    tm = x_ref.shape[1]

    @pl.when(t == 0)
    def _():
        carry_ref[...] = zp0_ref[0]

    xn = _rmsnorm(x_ref[0], g1_ref[...])
    zr = _dg(xn.astype(BF16), w_ref[...], NN)
    rows = lax.broadcasted_iota(jnp.int32, zr.shape, 0)
    zp = jnp.where(rows == 0, carry_ref[...], pltpu.roll(zr, 1, 0))
    carry_ref[...] = zr[tm - 1:tm, :]
    seg, exp = _seg_mats()
    outs = _prep_math(zr, zp, mu_ref[...], w0_ref[...], wdu_ref[...], a0_ref[...], aup_ref[...], gup_ref[...],
                      kkw_ref[...], kaw_ref[...], rkw_ref[...], seg, exp)
    for o_ref, val in zip((r_ref, lw_ref, k_ref, v_ref, kk_ref, b_ref, g_ref, bo_ref), outs):
        o_ref[0] = val
    xl_ref[0] = xn[tm - 1:tm, :]


def _prep_batch_kernel(x_ref, xp_ref, g1_ref, w_ref, mu_ref, w0_ref, wdu_ref, a0_ref, aup_ref, gup_ref,
                       kkw_ref, kaw_ref, rkw_ref,
                       r_ref, lw_ref, k_ref, v_ref, kk_ref, b_ref, g_ref, bo_ref, xn_ref):
    xn = _rmsnorm(x_ref[...], g1_ref[...])
    w = w_ref[...]
    zr = _dg(xn.astype(BF16), w, NN)
    zp = _dg(xp_ref[...].astype(BF16), w, NN)
    seg, exp = _seg_mats()
    outs = _prep_math(zr, zp, mu_ref[...], w0_ref[...], wdu_ref[...], a0_ref[...], aup_ref[...], gup_ref[...],
                      kkw_ref[...], kaw_ref[...], rkw_ref[...], seg, exp)
    for o_ref, val in zip((r_ref, lw_ref, k_ref, v_ref, kk_ref, b_ref, g_ref, bo_ref), outs):
        o_ref[...] = val
    xn_ref[...] = xn


def _prep_params(p):
    return (p['norm1_g'], p['w_rwkv'], p['mu_shift'], p['w0'], p['w_decay_up'], p['a0'], p['a_up'], p['g_up'],
            p['k_k'], p['k_a'], p['r_k'])


def _rwkv_prep_seq(x, zp0, p, tm):
    B, T, D = x.shape
    tm = min(tm, T)
    params = _prep_params(p)
    seq = pl.BlockSpec((1, tm, D), lambda b, t: (b, t, 0))
    out_shape = [jax.ShapeDtypeStruct((B, T, D), F32)] * 8 + [jax.ShapeDtypeStruct((B, 1, D), F32)]
    return pl.pallas_call(
        _prep_seq_kernel,
        grid=(B, T // tm),
        in_specs=[seq, pl.BlockSpec((1, 1, C_RWKV), lambda b, t: (b, 0, 0))] + [_full(a.shape) for a in params],
        out_specs=[seq] * 8 + [pl.BlockSpec((1, 1, D), lambda b, t: (b, 0, 0))],
        out_shape=out_shape,
        scratch_shapes=[pltpu.VMEM((1, C_RWKV), F32)],
        compiler_params=_cparams(("parallel", "arbitrary")),
        name="rwkv_prep_seq",
    )(x, zp0, *params)


def _rwkv_prep_batch(x, xprev, p):
    N, D = x.shape
    params = _prep_params(p)
    out_shape = [jax.ShapeDtypeStruct((N, D), F32)] * 9
    return pl.pallas_call(
        _prep_batch_kernel,
        grid=(1,),
        in_specs=[_full(x.shape), _full(xprev.shape)] + [_full(a.shape) for a in params],
        out_specs=[_full((N, D))] * 9,
        out_shape=out_shape,
        compiler_params=_cparams(("arbitrary",)),
        name="rwkv_prep_batch",
    )(x, xprev, *params)


HIST = 32


def _conv_tail(c, cb, lg, lb, mix_a, mix_b):
    c = c + cb
    mean = jnp.mean(c, axis=-1, keepdims=True)
    d = c - mean
    var = jnp.mean(d * d, axis=-1, keepdims=True)
    y = d * lax.rsqrt(var + LN_EPS) * lg + lb
    out_b = y * _sigmoid(y)
    return _sigmoid(mix_a), _sigmoid(mix_b) * out_b


def _glu_mix(xn, w_ref):
    xb = xn.astype(BF16)
    glu_a = _dg(xb, w_ref[:, 0:D_MODEL], NN)
    glu_b = _dg(xb, w_ref[:, D_MODEL:2 * D_MODEL], NN)
    mix_a = _dg(xb, w_ref[:, 2 * D_MODEL:3 * D_MODEL], NN)
    mix_b = _dg(xb, w_ref[:, 3 * D_MODEL:4 * D_MODEL], NN)
    return glu_a * _sigmoid(glu_b), mix_a, mix_b


def _conv_seq_kernel(x_ref, sc_ref, g1_ref, w_ref, cw_ref, cb_ref, lg_ref, lb_ref,
                     ga_ref, bm_ref, so_ref, ubuf_ref):
    t = pl.program_id(1)
    nt = pl.num_programs(1)
    tm = x_ref.shape[1]
    npast = CONV_WIDTH - 1

    @pl.when(t == 0)
    def _():
        ubuf_ref[pl.ds(0, HIST - npast), :] = jnp.zeros((HIST - npast, D_MODEL), F32)
        ubuf_ref[pl.ds(HIST - npast, npast), :] = sc_ref[0]

    @pl.when(t > 0)
    def _():
        ubuf_ref[pl.ds(0, HIST), :] = ubuf_ref[pl.ds(tm, HIST), :]

    xn = _rmsnorm(x_ref[0], g1_ref[...])
    u, mix_a, mix_b = _glu_mix(xn, w_ref)
    ubuf_ref[pl.ds(HIST, tm), :] = u
    c = cw_ref[pl.ds(npast, 1), :] * ubuf_ref[pl.ds(HIST, tm), :]
    for s in range(SUBLANES):
        offs = [o for o in range(HIST - npast, HIST) if o % SUBLANES == s]
        grp = None
        for o in offs:
            term = cw_ref[pl.ds(o - (HIST - npast), 1), :] * ubuf_ref[pl.ds(o - s, tm + SUBLANES), :]
            grp = term if grp is None else grp + term
        c = c + grp[s:s + tm, :]
    ga, bm = _conv_tail(c, cb_ref[...], lg_ref[...], lb_ref[...], mix_a, mix_b)
    ga_ref[0] = ga
    bm_ref[0] = bm

    @pl.when(t == nt - 1)
    def _():
        so_ref[0] = ubuf_ref[pl.ds(tm + HIST - npast, npast), :]


def _conv_batch_kernel(x_ref, sc_ref, g1_ref, w_ref, cw_ref, cb_ref, lg_ref, lb_ref,
                       ga_ref, bm_ref, so_ref):
    npast = CONV_WIDTH - 1
    xn = _rmsnorm(x_ref[...], g1_ref[...])
    u, mix_a, mix_b = _glu_mix(xn, w_ref)
    c = cw_ref[pl.ds(npast, 1), :] * u
    for j in range(npast):
        c = c + cw_ref[pl.ds(j, 1), :] * sc_ref[j]
    ga, bm = _conv_tail(c, cb_ref[...], lg_ref[...], lb_ref[...], mix_a, mix_b)
    ga_ref[...] = ga
    bm_ref[...] = bm
    for j in range(npast - 1):
        so_ref[j] = sc_ref[j + 1]
    so_ref[npast - 1] = u


def _conv_params(p):
    return (p['norm1_g'], p['w_rest'], p['conv_w'], p['conv_b'], p['cln_g'], p['cln_b'])


def _conv_branch_seq(x, s_conv, p, tm):
    B, T, D = x.shape
    tm = min(tm, T)
    params = _conv_params(p)
    npast = CONV_WIDTH - 1
    seq = pl.BlockSpec((1, tm, D), lambda b, t: (b, t, 0))
    st = pl.BlockSpec((1, npast, D), lambda b, t: (b, 0, 0))
    return pl.pallas_call(
        _conv_seq_kernel,
        grid=(B, T // tm),
        in_specs=[seq, st] + [_full(a.shape) for a in params],
        out_specs=[seq, seq, st],
        out_shape=[jax.ShapeDtypeStruct((B, T, D), F32)] * 2 + [jax.ShapeDtypeStruct((B, npast, D), F32)],
        scratch_shapes=[pltpu.VMEM((tm + HIST, D), F32)],
        compiler_params=_cparams(("parallel", "arbitrary")),
        name="conv_branch_seq",
    )(x, s_conv, *params)


def _conv_branch_batch(x, s_conv_t, p):
    N, D = x.shape
    params = _conv_params(p)
    return pl.pallas_call(
        _conv_batch_kernel,
        grid=(1,),
        in_specs=[_full(x.shape), _full(s_conv_t.shape)] + [_full(a.shape) for a in params],
        out_specs=[_full((N, D)), _full((N, D)), _full(s_conv_t.shape)],
        out_shape=[jax.ShapeDtypeStruct((N, D), F32)] * 2 + [jax.ShapeDtypeStruct(s_conv_t.shape, F32)],
        compiler_params=_cparams(("arbitrary",)),
        name="conv_branch_batch",
    )(x, s_conv_t, *params)


def _pair_masks(shape):
    lane = lax.broadcasted_iota(jnp.int32, shape, 1)
    return lane < HEAD_DIM


def _bd(y, m0):
    zero = jnp.zeros_like(y)
    return jnp.concatenate([jnp.where(m0, y, zero), jnp.where(m0, zero, y)], axis=0)


def _bdmm(x, y, m0, mm):
    return mm(x, _bd(y, m0), NN)


def _bdmm_nt(x, y, m0, mm):
    return mm(x, _bd(y, m0), NT)


def _bdmm_tn(x, y, m0, mm):
    a = mm(x, y, TN)
    return jnp.where(m0, a[0:HEAD_DIM, :], a[HEAD_DIM:PAIR, :])


def _map(f, *lists):
    return [f(*xs) for xs in zip(*lists)]


def _chunk_pairs(r, cum, lw, k, v, kk, b):
    L = CHUNK
    m0 = _pair_masks((L, PAIR))
    trow = lax.broadcasted_iota(jnp.int32, (L, PAIR), 0)
    icol = lax.broadcasted_iota(jnp.int32, (L, PAIR), 1) % HEAD_DIM
    strict = icol < trow
    incl = icol <= trow
    eye = (icol == trow).astype(F32)
    zero = jnp.zeros((L, PAIR), F32)

    cl = [c[L - 1:L, :] for c in cum]
    alpha = _map(lambda x, c, w: x * jnp.exp(c - w), kk, cum, lw)
    rho = _map(lambda x, c: x * jnp.exp(c), r, cum)
    einv = [jnp.exp(-c) for c in cum]
    kappa = _map(lambda x, e: x * e, k, einv)
    beta = _map(lambda x, e: x * e, b, einv)
    etail = _map(lambda c1, c: jnp.exp(c1 - c), cl, cum)
    kappa2 = _map(lambda x, e: x * e, k, etail)
    beta2 = _map(lambda x, e: x * e, b, etail)
    dl = [jnp.exp(c1) for c1 in cl]

    mm = _mm1

    def rows2(x0, x1, y, dims):
        out = mm(jnp.concatenate([x0, x1], axis=0), _bd(y, m0), dims)
        return out[0:L, :], out[L:2 * L, :]

    sk = _map(lambda a, q, y: rows2(a, q, y, NT), alpha, rho, kappa)
    sb = _map(lambda a, q, y: rows2(a, q, y, NT), alpha, rho, beta)
    m_k = [jnp.where(strict, x[0], zero) for x in sk]
    n_k = [jnp.where(incl, x[1], zero) for x in sk]
    m_b = [jnp.where(strict, x[0], zero) for x in sb]
    n_b = [jnp.where(incl, x[1], zero) for x in sb]

    nn = [-m for m in m_b]
    tinv = [eye + q for q in nn]
    pw = _map(lambda q: _bdmm(q, q, m0, mm), nn)
    mvn = _map(lambda a, q, y: rows2(a, q, y, NN), m_k, n_k, v)
    for _ in range(4):
        res = _map(lambda q, t: rows2(q, t, q, NN), pw, tinv)
        tinv = _map(lambda t, x: t + x[1], tinv, res)
        pw = [x[0] for x in res]
    tinv = _map(lambda t, q: t + _bdmm(t, q, m0, mm), tinv, pw)

    mv = [x[0] for x in mvn]
    nkv = [x[1] for x in mvn]
    alpha2 = _map(lambda t, x: _bdmm(t, x, m0, mm), tinv, alpha)
    w = _map(lambda t, x: _bdmm(t, x, m0, mm), tinv, mv)
    rho2 = _map(lambda x, n, a2: x - _bdmm(n, a2, m0, mm), rho, n_b, alpha2)
    o2 = _map(lambda x, n, y: x - _bdmm(n, y, m0, mm), nkv, n_b, w)

    def tn2(x0, x1, y):
        a = mm(jnp.concatenate([x0, x1], axis=1), y, TN)
        pick = lambda z: jnp.where(m0, z[0:HEAD_DIM, :], z[HEAD_DIM:PAIR, :])
        return pick(a[0:PAIR, :]), pick(a[PAIR:2 * PAIR, :])

    ab = _map(tn2, alpha2, w, beta2)
    g = _map(lambda d, x: eye * d - x[0], dl, ab)
    h = _map(lambda x, k2, y: _bdmm_tn(x, k2, m0, mm) - y[1], v, kappa2, ab)
    return rho2, o2, g, h


def _wkv_pre_kernel(r_ref, lw_ref, k_ref, v_ref, kk_ref, b_ref, rho_ref, o_ref, g_ref, h_ref):
    L = CHUNK
    width = r_ref.shape[2]
    ti = lax.broadcasted_iota(jnp.int32, (L, L), 0)
    tj = lax.broadcasted_iota(jnp.int32, (L, L), 1)
    ltri = (tj <= ti).astype(BF16)
    lw_all = lw_ref[0]
    h3, m3, l3 = _split3(lw_all)
    cum_all = _dg(ltri, h3, NN) + (_dg(ltri, m3, NN) + _dg(ltri, l3, NN))
    sls = [slice(p * PAIR, (p + 1) * PAIR) for p in range(width // PAIR)]
    pick = lambda ref: [ref[0, :, sl] for sl in sls]
    outs = _chunk_pairs(pick(r_ref), [cum_all[:, sl] for sl in sls], [lw_all[:, sl] for sl in sls],
                        pick(k_ref), pick(v_ref), pick(kk_ref), pick(b_ref))
    for ref, vals in zip((rho_ref, o_ref, g_ref, h_ref), outs):
        for sl, val in zip(sls, vals):
            ref[0, :, sl] = val


def _wkv_pre(r, lw, k, v, kk, b, width):
    B, T, D = r.shape
    blk = pl.BlockSpec((1, CHUNK, width), lambda bi, c, w: (bi, c, w))
    return pl.pallas_call(
        _wkv_pre_kernel,
        grid=(B, T // CHUNK, D // width),
        in_specs=[blk] * 6,
        out_specs=[blk] * 4,
        out_shape=[jax.ShapeDtypeStruct((B, T, D), F32)] * 4,
        compiler_params=_cparams(("parallel", "parallel", "parallel")),
        name="wkv_pre",
    )(r, lw, k, v, kk, b)


def _wkv_seq_kernel(rho_ref, o2_ref, g_ref, h_ref, o_ref, s_out_ref, s_ref):
    c = pl.program_id(1)
    nc = pl.num_programs(1)
    m0 = _pair_masks((CHUNK, PAIR))

    @pl.when(c == 0)
    def _():
        s_ref[...] = jnp.zeros_like(s_ref)

    sls = [slice(p * PAIR, (p + 1) * PAIR) for p in range(D_MODEL // PAIR)]
    s = [s_ref[:, sl] for sl in sls]
    s_new = [_bdmm(x, g_ref[0, :, sl], m0, _mm3) + h_ref[0, :, sl] for sl, x in zip(sls, s)]
    o = [_bdmm_nt(rho_ref[0, :, sl], x, m0, _mm1) + o2_ref[0, :, sl] for sl, x in zip(sls, s)]
    for sl, ov, sv in zip(sls, o, s_new):
        s_ref[:, sl] = sv
        o_ref[0, :, sl] = ov

    @pl.when(c == nc - 1)
    def _():
        for hd in range(N_HEADS):
            s_out_ref[0, hd] = s_ref[:, hd * HEAD_DIM:(hd + 1) * HEAD_DIM]


def _wkv_seq(rho2, o2, g, h):
    B, T, D = rho2.shape
    blk = pl.BlockSpec((1, CHUNK, D), lambda bi, c: (bi, c, 0))
    return pl.pallas_call(
        _wkv_seq_kernel,
        grid=(B, T // CHUNK),
        in_specs=[blk] * 4,
        out_specs=[blk, pl.BlockSpec((1, N_HEADS, HEAD_DIM, HEAD_DIM), lambda bi, c: (bi, 0, 0, 0))],
        out_shape=[jax.ShapeDtypeStruct((B, T, D), F32),
                   jax.ShapeDtypeStruct((B, N_HEADS, HEAD_DIM, HEAD_DIM), F32)],
        scratch_shapes=[pltpu.VMEM((HEAD_DIM, D), F32)],
        compiler_params=_cparams(("parallel", "arbitrary")),
        name="wkv_seq",
    )(rho2, o2, g, h)


def _wkv_step_kernel(s_ref, r_ref, lw_ref, k_ref, v_ref, kk_ref, b_ref, o_ref, so_ref):
    nb = s_ref.shape[0]
    npair = N_HEADS // 2
    lane = lax.broadcasted_iota(jnp.int32, (HEAD_DIM, PAIR), 1)
    rowi = lax.broadcasted_iota(jnp.int32, (HEAD_DIM, PAIR), 0)
    eyep = (lane % HEAD_DIM == rowi).astype(F32)
    bi = lax.broadcasted_iota(jnp.int32, (PAIR, PAIR), 0) // HEAD_DIM
    bj = lax.broadcasted_iota(jnp.int32, (PAIR, PAIR), 1) // HEAD_DIM
    ones2 = (bi == bj).astype(BF16)
    items = [(i, p) for i in range(nb) for p in range(npair)]
    n = len(items)
    row = lambda ref, i, p: ref[pl.ds(i, 1), p * PAIR:(p + 1) * PAIR]
    sp = [jnp.concatenate([s_ref[i, 2 * p], s_ref[i, 2 * p + 1]], axis=1) for i, p in items]
    stack = jnp.concatenate([s * row(kk_ref, i, p) for s, (i, p) in zip(sp, items)]
                            + [eyep * row(v_ref, i, p) for i, p in items], axis=0)
    red = _mm_exact_rhs(stack, ones2)
    blk = lambda a, j: a[j * HEAD_DIM:(j + 1) * HEAD_DIM, :]
    s_new = [s * jnp.exp(row(lw_ref, i, p)) - blk(red, j) * row(b_ref, i, p) + blk(red, n + j) * row(k_ref, i, p)
             for j, (s, (i, p)) in enumerate(zip(sp, items))]
    for s, (i, p) in zip(s_new, items):
        so_ref[i, 2 * p] = s[:, :HEAD_DIM]
        so_ref[i, 2 * p + 1] = s[:, HEAD_DIM:]
    ored = _mm_exact_rhs(jnp.concatenate([s * row(r_ref, i, p) for s, (i, p) in zip(s_new, items)], axis=0), ones2)
    o_row = [jnp.sum(eyep * blk(ored, j), axis=0, keepdims=True) for j in range(n)]
    for i in range(nb):
        o_ref[pl.ds(i, 1), :] = jnp.concatenate(o_row[i * npair:(i + 1) * npair], axis=1)


def _wkv_step(s, r, lw, k, v, kk, b, nb):
    N = s.shape[0]
    nb = min(nb, N)
    st = pl.BlockSpec((nb, N_HEADS, HEAD_DIM, HEAD_DIM), lambda i: (i, 0, 0, 0))
    vec = pl.BlockSpec((nb, D_MODEL), lambda i: (i, 0))
    return pl.pallas_call(
        _wkv_step_kernel,
        grid=(N // nb,),
        in_specs=[st] + [vec] * 6,
        out_specs=[vec, st],
        out_shape=[jax.ShapeDtypeStruct((N, D_MODEL), F32), jax.ShapeDtypeStruct(s.shape, F32)],
        compiler_params=_cparams(("parallel",)),
        name="wkv_step",
    )(s, r, lw, k, v, kk, b)


def _first_max(x, axis, n):
    m = jnp.max(x, axis=axis, keepdims=True)
    idx = lax.broadcasted_iota(jnp.int32, x.shape, axis)
    first = jnp.min(jnp.where(x == m, idx, n), axis=axis, keepdims=True)
    return m, idx == first


def _route(scores, biased):
    tm = scores.shape[1]
    per = N_EXPERTS // N_GROUPS
    neg = jnp.full((), -jnp.inf, F32)
    b3 = biased.reshape(N_GROUPS, per, tm)
    m1, hit = _first_max(b3, 1, per)
    m2 = jnp.max(jnp.where(hit, neg, b3), axis=1, keepdims=True)
    gs = (m1 + m2).reshape(N_GROUPS, tm)
    gsel = jnp.zeros((N_GROUPS, tm), jnp.bool_)
    for _ in range(TOPK_GROUPS):
        _, hit = _first_max(gs, 0, N_GROUPS)
        gsel = jnp.logical_or(gsel, hit)
        gs = jnp.where(hit, neg, gs)
    emask = jnp.broadcast_to(gsel.reshape(N_GROUPS, 1, tm), (N_GROUPS, per, tm)).reshape(N_EXPERTS, tm)
    cand = jnp.where(emask, biased, neg)
    esel = jnp.zeros((N_EXPERTS, tm), jnp.bool_)
    for _ in range(TOP_K):
        _, hit = _first_max(cand, 0, N_EXPERTS)
        esel = jnp.logical_or(esel, hit)
        cand = jnp.where(hit, neg, cand)
    wsel = jnp.where(esel, scores, 0.0)
    return wsel / jnp.sum(wsel, axis=0, keepdims=True) * ROUTED_SCALE


def _post_kernel(o_ref, g_ref, bo_ref, ga_ref, bm_ref, x_ref,
                 gng_ref, gnb_ref, wo_ref, n2_ref, wsg_ref, wsu_ref, wsd_ref, wrt_ref, eb_ref,
                 base_ref, hn_ref, gate_ref):
    seg, exp = _seg_mats()
    o = o_ref[...]
    inv_n = 1.0 / HEAD_DIM
    mean = _mm_exact_rhs(_mm_exact_rhs(o, seg, passes=2) * inv_n, exp, passes=2)
    d = o - mean
    var = _mm_exact_rhs(d * d, seg, passes=1) * inv_n
    rstd = _mm_exact_rhs(lax.rsqrt(var + GN_EPS), exp, passes=1)
    ogn = d * rstd * gng_ref[...] + gnb_ref[...]
    out_a = (ogn + bo_ref[...]) * g_ref[...]
    merged = ga_ref[...] * out_a + bm_ref[...]
    h = x_ref[...] + _dg(merged.astype(BF16), wo_ref[...], NN)
    hn = _rmsnorm(h, n2_ref[...])
    hb = hn.astype(BF16)
    sg = _dg(hb, wsg_ref[...], NN)
    su = _dg(hb, wsu_ref[...], NN)
    shared = _dg((sg * _sigmoid(sg) * su).astype(BF16), wsd_ref[...], NN)
    base_ref[...] = h + shared
    hn_ref[...] = hb
    logits = _mm3(wrt_ref[...], hn, NT)
    scores = _sigmoid(logits)
    gate_t = _route(scores, scores + eb_ref[...])
    tm = gate_t.shape[1]
    gate_pad = jnp.concatenate([gate_t, jnp.zeros((LANES - N_EXPERTS, tm), F32)], axis=0)
    gate_ref[...] = gate_pad.T


def _post(o, g, bonus, ga, bm, x, p, tm):
    N, D = x.shape
    tm = min(tm, N)
    params = (p['gn_g'], p['gn_b'], p['w_out'], p['norm2_g'], p['ws_gate'], p['ws_up'], p['ws_down'],
              p['w_router_t'], p['e_bias'])
    row = pl.BlockSpec((tm, D), lambda i: (i, 0))
    return pl.pallas_call(
        _post_kernel,
        grid=(N // tm,),
        in_specs=[row] * 6 + [_full(a.shape) for a in params],
        out_specs=[row, row, pl.BlockSpec((tm, LANES), lambda i: (i, 0))],
        out_shape=[jax.ShapeDtypeStruct((N, D), F32), jax.ShapeDtypeStruct((N, D), BF16),
                   jax.ShapeDtypeStruct((N, LANES), F32)],
        compiler_params=_cparams(("parallel",)),
        name="post",
    )(o, g, bonus, ga, bm, x, *params)


def _moe_kernel(x_ref, gate_ref, base_ref, wg_ref, wu_ref, wd_ref, nf_ref, y_ref):
    j = pl.program_id(1)
    nj = pl.num_programs(1)
    eps = wg_ref.shape[0]
    gate = gate_ref[...]
    lane = lax.broadcasted_iota(jnp.int32, gate.shape, 1)
    x = x_ref[...]
    cols = [jnp.sum(jnp.where(lane == j * eps + q, gate, 0.0), axis=1, keepdims=True) for q in range(eps)]
    hg = [_dg(x, wg_ref[q], NN) for q in range(eps)]
    hu = [_dg(x, wu_ref[q], NN) for q in range(eps)]
    hh = [(hg[q] * _sigmoid(hg[q]) * hu[q] * cols[q]).astype(BF16) for q in range(eps)]
    down = _dg(jnp.concatenate(hh, axis=1), wd_ref[...].reshape(eps * D_EXPERT, D_MODEL), NN)

    @pl.when(j == 0)
    def _():
        y_ref[...] = base_ref[...] + down

    @pl.when(j > 0)
    def _():
        y_ref[...] += down

    @pl.when(j == nj - 1)
    def _():
        y_ref[...] = _rmsnorm(y_ref[...], nf_ref[...])


def _moe(hn, gate, base, p, tm, eps):
    N, D = base.shape
    tm = min(tm, N)
    row = lambda w: pl.BlockSpec((tm, w), lambda i, e: (i, 0))
    return pl.pallas_call(
        _moe_kernel,
        grid=(N // tm, N_EXPERTS // eps),
        in_specs=[row(D), row(LANES), row(D),
                  pl.BlockSpec((eps, D, D_EXPERT), lambda i, e: (e, 0, 0)),
                  pl.BlockSpec((eps, D, D_EXPERT), lambda i, e: (e, 0, 0)),
                  pl.BlockSpec((eps, D_EXPERT, D), lambda i, e: (e, 0, 0)),
                  pl.BlockSpec((1, D), lambda i, e: (0, 0))],
        out_specs=row(D),
        out_shape=jax.ShapeDtypeStruct((N, D), F32),
        compiler_params=_cparams(("parallel", "arbitrary")),
        name="moe",
    )(hn, gate, base, p['w_gate'], p['w_up'], p['w_down'], p['normf_g'])


def kernel(x_prompt, x_sample, state_wkv, state_shift, state_conv, norm1_g, w_in, mu_shift, w0, w_decay_up, a0, a_up, g_up, k_k, k_a, r_k, gn_g, gn_b, conv_w, conv_b, cln_g, cln_b, w_out, norm2_g, w_router, e_bias, w_gate, w_up, w_down, ws_gate, ws_up, ws_down, normf_g):
    depth = w_in.shape[0]
    assert depth == 1
    B, T, D = x_prompt.shape
    NS = x_sample.shape[0]
    assert x_sample.shape[1] == 1 and D == D_MODEL and T % CHUNK == 0
    row = lambda a: a[0].reshape(1, -1)
    p = {
        'norm1_g': row(norm1_g), 'mu_shift': row(mu_shift), 'w0': row(w0), 'a0': row(a0),
        'k_k': row(k_k), 'k_a': row(k_a), 'r_k': row(r_k), 'gn_g': row(gn_g), 'gn_b': row(gn_b),
        'conv_b': row(conv_b), 'cln_g': row(cln_g), 'cln_b': row(cln_b), 'norm2_g': row(norm2_g),
        'normf_g': normf_g.reshape(1, -1),
        'w_rwkv': w_in[0, :, :C_RWKV].astype(BF16), 'w_rest': w_in[0, :, C_RWKV:].astype(BF16),
        'w_decay_up': w_decay_up[0], 'a_up': a_up[0], 'g_up': g_up[0], 'conv_w': conv_w[0],
        'w_out': w_out[0].astype(BF16),
        'ws_gate': ws_gate[0].astype(BF16), 'ws_up': ws_up[0].astype(BF16), 'ws_down': ws_down[0].astype(BF16),
        'w_router_t': w_router[0].T, 'e_bias': e_bias[0].reshape(-1, 1),
        'w_gate': w_gate[0].astype(BF16), 'w_up': w_up[0].astype(BF16), 'w_down': w_down[0].astype(BF16),
    }

    zp0 = jnp.zeros((B, 1, C_RWKV), F32)
    r, lw, k, v, kk, b, g, bonus, shift_p = _rwkv_prep_seq(x_prompt, zp0, p, tm=256)
    ga, bm, conv_p = _conv_branch_seq(x_prompt, jnp.zeros((B, CONV_WIDTH - 1, D), F32), p, tm=256)
    rho2, o2, gm, hm = _wkv_pre(r, lw, k, v, kk, b, width=D_MODEL)
    o, wkv_p = _wkv_seq(rho2, o2, gm, hm)
    flat = lambda a: a.reshape(B * T, D)
    base, hn, gate = _post(flat(o), flat(g), flat(bonus), flat(ga), flat(bm), flat(x_prompt), p, tm=256)
    y_prompt = _moe(hn, gate, base, p, tm=1024, eps=4).reshape(B, T, D)

    xs = x_sample.reshape(NS, D)
    r, lw, k, v, kk, b, g, bonus, shift_s = _rwkv_prep_batch(xs, state_shift[0], p)
    ga, bm, conv_s_t = _conv_branch_batch(xs, jnp.swapaxes(state_conv[0], 0, 1), p)
    o, wkv_s = _wkv_step(state_wkv[0], r, lw, k, v, kk, b, nb=8)
    base, hn, gate = _post(o, g, bonus, ga, bm, xs, p, tm=128)
    y_sample = _moe(hn, gate, base, p, tm=128, eps=4).reshape(NS, 1, D)

    return (y_prompt, y_sample, wkv_p[None], shift_p.reshape(1, B, D), conv_p[None],
            wkv_s[None], shift_s[None], jnp.swapaxes(conv_s_t, 0, 1)[None])
```

```python
import functools

import jax
import jax.numpy as jnp
from jax import lax
from jax.experimental import pallas as pl
from jax.experimental.pallas import tpu as pltpu

F32 = jnp.float32
BF16 = jnp.bfloat16

D_MODEL = 1024
HEAD_DIM = 64
N_HEADS = D_MODEL // HEAD_DIM
D_DECAY_LORA = 64
D_AAA_LORA = 64
D_GATE_LORA = 128
GN_EPS = 64e-5
CONV_WIDTH = 31
LN_EPS = 1e-5
N_EXPERTS = 64
N_GROUPS = 8
TOPK_GROUPS = 4
TOP_K = 8
D_EXPERT = 256
ROUTED_SCALE = 2.5
RMS_EPS = 1e-6

O_K = D_MODEL
O_V = 2 * D_MODEL
O_W = 3 * D_MODEL
O_A = O_W + D_DECAY_LORA
O_G = O_A + D_AAA_LORA
C_RWKV = O_G + D_GATE_LORA
C_REST = 4 * D_MODEL

LANES = 128
SUBLANES = 8
CHUNK = 64
PAIR = 2 * HEAD_DIM
VMEM_LIMIT = 56 * 1024 * 1024

NN = ((1,), (0,))
NT = ((1,), (1,))
TN = ((0,), (0,))


def _dg(a, b, dims):
    return lax.dot_general(a, b, (dims, ((), ())), preferred_element_type=F32)


def _split2(x):
    hi = x.astype(BF16)
    lo = (x - hi.astype(F32)).astype(BF16)
    return hi, lo


def _split3(x):
    hi = x.astype(BF16)
    r1 = x - hi.astype(F32)
    mid = r1.astype(BF16)
    lo = (r1 - mid.astype(F32)).astype(BF16)
    return hi, mid, lo


def _mm1(a, b, dims=NN):
    return _dg(a.astype(BF16), b.astype(BF16), dims)


def _mm3(a, b, dims=NN):
    ah, al = _split2(a)
    bh, bl = _split2(b)
    return _dg(ah, bh, dims) + (_dg(ah, bl, dims) + _dg(al, bh, dims))


def _mm_exact_rhs(a, b_bf16, dims=NN, passes=3):
    if passes == 1:
        return _dg(a.astype(BF16), b_bf16, dims)
    if passes == 2:
        h, l = _split2(a)
        return _dg(h, b_bf16, dims) + _dg(l, b_bf16, dims)
    h, m, l = _split3(a)
    return _dg(h, b_bf16, dims) + (_dg(m, b_bf16, dims) + _dg(l, b_bf16, dims))


def _rmsnorm(x, g):
    return x * lax.rsqrt(jnp.mean(x * x, axis=-1, keepdims=True) + RMS_EPS) * g


def _sigmoid(x):
    return 1.0 / (1.0 + jnp.exp(-x))


def _seg_mats():
    row = lax.broadcasted_iota(jnp.int32, (D_MODEL, LANES), 0) // HEAD_DIM
    col = lax.broadcasted_iota(jnp.int32, (D_MODEL, LANES), 1)
    seg = (row == col).astype(BF16)
    rowt = lax.broadcasted_iota(jnp.int32, (LANES, D_MODEL), 0)
    colt = lax.broadcasted_iota(jnp.int32, (LANES, D_MODEL), 1) // HEAD_DIM
    exp = (rowt == colt).astype(BF16)
    return seg, exp


def _cparams(sem):
    return pltpu.CompilerParams(dimension_semantics=sem, vmem_limit_bytes=VMEM_LIMIT)


def _full(shape):
    n = len(shape)
    return pl.BlockSpec(shape, lambda *_: (0,) * n)


def _prep_math(zr, zp, mu, w0, wdu, a0, aup, gup, kk_w, ka_w, rk_w, seg, exp):
    zm = zr + (zp - zr) * mu
    r = zm[:, 0:O_K]
    k = zm[:, O_K:O_V]
    v = zm[:, O_V:O_W]
    xw = jnp.tanh(zm[:, O_W:O_A])
    xa = zm[:, O_A:O_G]
    xg = _sigmoid(zm[:, O_G:C_RWKV])
    y = -(w0 + _mm1(xw, wdu))
    softplus = jnp.maximum(y, 0.0) + jnp.log(1.0 + jnp.exp(-jnp.abs(y)))
    lw = -jnp.exp(-softplus - 0.5)
    a = _sigmoid(a0 + _mm1(xa, aup))
    g = _mm1(xg, gup)
    kkr = k * kk_w
    ss = _mm_exact_rhs(kkr * kkr, seg, passes=2)
    inv = 1.0 / jnp.maximum(jnp.sqrt(ss), 1e-12)
    kk = kkr * _mm_exact_rhs(inv, exp, passes=2)
    kf = k * (1.0 + (a - 1.0) * ka_w)
    b = kk * a
    rk = _mm_exact_rhs(r * kf * rk_w, seg, passes=1)
    bonus = _mm_exact_rhs(rk, exp, passes=1) * v
    return r, lw, kf, v, kk, b, g, bonus


def _prep_seq_body(zr, vec_refs, carry_ref):
    tm = zr.shape[0]
    rows = lax.broadcasted_iota(jnp.int32, zr.shape, 0)
    zp = jnp.where(rows == 0, carry_ref[...], pltpu.roll(zr, 1, 0))
    carry_ref[...] = zr[tm - 1:tm, :]
    seg, exp = _seg_mats()
    return _prep_math(zr, zp, *[ref[...] for ref in vec_refs], seg, exp)


def _prep_batch_kernel(x_ref, xp_ref, g1_ref, w_ref, mu_ref, w0_ref, wdu_ref, a0_ref, aup_ref, gup_ref,
                       kkw_ref, kaw_ref, rkw_ref,
                       r_ref, lw_ref, k_ref, v_ref, kk_ref, b_ref, g_ref, bo_ref, xn_ref):
    xn = _rmsnorm(x_ref[...], g1_ref[...])
    w = w_ref[...]
    zr = _dg(xn.astype(BF16), w, NN)
    zp = _dg(xp_ref[...].astype(BF16), w, NN)
    seg, exp = _seg_mats()
    outs = _prep_math(zr, zp, mu_ref[...], w0_ref[...], wdu_ref[...], a0_ref[...], aup_ref[...], gup_ref[...],
                      kkw_ref[...], kaw_ref[...], rkw_ref[...], seg, exp)
    for o_ref, val in zip((r_ref, lw_ref, k_ref, v_ref, kk_ref, b_ref, g_ref, bo_ref), outs):
        o_ref[...] = val
    xn_ref[...] = xn


def _prep_params(p):
    return (p['norm1_g'], p['w_rwkv'], p['mu_shift'], p['w0'], p['w_decay_up'], p['a0'], p['a_up'], p['g_up'],
            p['k_k'], p['k_a'], p['r_k'])


def _rwkv_prep_batch(x, xprev, p):
    N, D = x.shape
    params = _prep_params(p)
    out_shape = [jax.ShapeDtypeStruct((N, D), F32)] * 9
    return pl.pallas_call(
        _prep_batch_kernel,
        grid=(1,),
        in_specs=[_full(x.shape), _full(xprev.shape)] + [_full(a.shape) for a in params],
        out_specs=[_full((N, D))] * 9,
        out_shape=out_shape,
        compiler_params=_cparams(("arbitrary",)),
        name="rwkv_prep_batch",
    )(x, xprev, *params)


HIST = 32


def _conv_tail(c, cb, lg, lb, mix_a, mix_b):
    c = c + cb
    mean = jnp.mean(c, axis=-1, keepdims=True)
    d = c - mean
    var = jnp.mean(d * d, axis=-1, keepdims=True)
    y = d * lax.rsqrt(var + LN_EPS) * lg + lb
    out_b = y * _sigmoid(y)
    return _sigmoid(mix_a), _sigmoid(mix_b) * out_b


def _glu_mix(xn, w_ref):
    xb = xn.astype(BF16)
    glu_a = _dg(xb, w_ref[:, 0:D_MODEL], NN)
    glu_b = _dg(xb, w_ref[:, D_MODEL:2 * D_MODEL], NN)
    mix_a = _dg(xb, w_ref[:, 2 * D_MODEL:3 * D_MODEL], NN)
    mix_b = _dg(xb, w_ref[:, 3 * D_MODEL:4 * D_MODEL], NN)
    return glu_a * _sigmoid(glu_b), mix_a, mix_b


def _conv_taps(u, cw_ref, ubuf_ref):
    tm = u.shape[0]
    npast = CONV_WIDTH - 1
    ubuf_ref[pl.ds(HIST, tm), :] = u
    c = cw_ref[pl.ds(npast, 1), :] * ubuf_ref[pl.ds(HIST, tm), :]
    for s in range(SUBLANES):
        offs = [o for o in range(HIST - npast, HIST) if o % SUBLANES == s]
        grp = None
        for o in offs:
            term = cw_ref[pl.ds(o - (HIST - npast), 1), :] * ubuf_ref[pl.ds(o - s, tm + SUBLANES), :]
            grp = term if grp is None else grp + term
        c = c + grp[s:s + tm, :]
    return c


def _mix_seq_kernel(x_ref, zp0_ref, sc_ref, g1_ref, wr_ref, mu_ref, w0_ref, wdu_ref, a0_ref, aup_ref, gup_ref,
                    kkw_ref, kaw_ref, rkw_ref, wc_ref, cw_ref, cb_ref, lg_ref, lb_ref,
                    r_ref, lw_ref, k_ref, v_ref, kk_ref, b_ref, g_ref, bo_ref, xl_ref, ga_ref, bm_ref, so_ref,
                    carry_ref, ubuf_ref):
    t = pl.program_id(1)
    nt = pl.num_programs(1)
    tm = x_ref.shape[1]
    npast = CONV_WIDTH - 1

    @pl.when(t == 0)
    def _():
        carry_ref[...] = zp0_ref[0]
        ubuf_ref[pl.ds(0, HIST - npast), :] = jnp.zeros((HIST - npast, D_MODEL), F32)
        ubuf_ref[pl.ds(HIST - npast, npast), :] = sc_ref[0]

    @pl.when(t > 0)
    def _():
        ubuf_ref[pl.ds(0, HIST), :] = ubuf_ref[pl.ds(tm, HIST), :]

    xn = _rmsnorm(x_ref[0], g1_ref[...])
    xb = xn.astype(BF16)
    glu_a = _dg(xb, wc_ref[:, 0:D_MODEL], NN)
    glu_b = _dg(xb, wc_ref[:, D_MODEL:2 * D_MODEL], NN)
    c = _conv_taps(glu_a * _sigmoid(glu_b), cw_ref, ubuf_ref)
    zr = _dg(xb, wr_ref[...], NN)
    mix_a = _dg(xb, wc_ref[:, 2 * D_MODEL:3 * D_MODEL], NN)
    mix_b = _dg(xb, wc_ref[:, 3 * D_MODEL:4 * D_MODEL], NN)
    ga, bm = _conv_tail(c, cb_ref[...], lg_ref[...], lb_ref[...], mix_a, mix_b)
    outs = _prep_seq_body(zr, (mu_ref, w0_ref, wdu_ref, a0_ref, aup_ref, gup_ref, kkw_ref, kaw_ref, rkw_ref), carry_ref)
    for o_ref, val in zip((r_ref, lw_ref, k_ref, v_ref, kk_ref, b_ref, g_ref, bo_ref), outs):
        o_ref[0] = val
    xl_ref[0] = xn[tm - 1:tm, :]
    ga_ref[0] = ga
    bm_ref[0] = bm

    @pl.when(t == nt - 1)
    def _():
        so_ref[0] = ubuf_ref[pl.ds(tm + HIST - npast, npast), :]


def _conv_batch_kernel(x_ref, sc_ref, g1_ref, w_ref, cw_ref, cb_ref, lg_ref, lb_ref,
                       ga_ref, bm_ref, so_ref):
    npast = CONV_WIDTH - 1
    xn = _rmsnorm(x_ref[...], g1_ref[...])
    u, mix_a, mix_b = _glu_mix(xn, w_ref)
    c = cw_ref[pl.ds(npast, 1), :] * u
    for j in range(npast):
        c = c + cw_ref[pl.ds(j, 1), :] * sc_ref[j]
    ga, bm = _conv_tail(c, cb_ref[...], lg_ref[...], lb_ref[...], mix_a, mix_b)
    ga_ref[...] = ga
    bm_ref[...] = bm
    for j in range(npast - 1):
        so_ref[j] = sc_ref[j + 1]
    so_ref[npast - 1] = u


def _conv_params(p):
    return (p['norm1_g'], p['w_rest'], p['conv_w'], p['conv_b'], p['cln_g'], p['cln_b'])


def _resident(shape):
    n = len(shape)
    return pl.BlockSpec(shape, lambda *_: (0,) * n, pipeline_mode=pl.Buffered(1))


def _mix_seq(x, zp0, s_conv, p, tm):
    B, T, D = x.shape
    tm = min(tm, T)
    params = _prep_params(p) + _conv_params(p)[1:]
    npast = CONV_WIDTH - 1
    seq = pl.BlockSpec((1, tm, D), lambda b, t: (b, t, 0))
    per_b = lambda n: pl.BlockSpec((1, n, D), lambda b, t: (b, 0, 0))
    return pl.pallas_call(
        _mix_seq_kernel,
        grid=(B, T // tm),
        in_specs=[seq, pl.BlockSpec((1, 1, C_RWKV), lambda b, t: (b, 0, 0)), per_b(npast)]
                 + [_resident(a.shape) for a in params],
        out_specs=[seq] * 8 + [per_b(1), seq, seq, per_b(npast)],
        out_shape=[jax.ShapeDtypeStruct((B, T, D), F32)] * 8 + [jax.ShapeDtypeStruct((B, 1, D), F32)]
                  + [jax.ShapeDtypeStruct((B, T, D), F32)] * 2 + [jax.ShapeDtypeStruct((B, npast, D), F32)],
        scratch_shapes=[pltpu.VMEM((1, C_RWKV), F32), pltpu.VMEM((tm + HIST, D), F32)],
        compiler_params=_cparams(("parallel", "arbitrary")),
        name="mix_seq",
    )(x, zp0, s_conv, *params)


def _conv_branch_batch(x, s_conv_t, p):
    N, D = x.shape
    params = _conv_params(p)
    return pl.pallas_call(
        _conv_batch_kernel,
        grid=(1,),
        in_specs=[_full(x.shape), _full(s_conv_t.shape)] + [_full(a.shape) for a in params],
        out_specs=[_full((N, D)), _full((N, D)), _full(s_conv_t.shape)],
        out_shape=[jax.ShapeDtypeStruct((N, D), F32)] * 2 + [jax.ShapeDtypeStruct(s_conv_t.shape, F32)],
        compiler_params=_cparams(("arbitrary",)),
        name="conv_branch_batch",
    )(x, s_conv_t, *params)


def _pair_masks(shape):
    lane = lax.broadcasted_iota(jnp.int32, shape, 1)
    return lane < HEAD_DIM


def _bd(y, m0):
    zero = jnp.zeros_like(y)
    return jnp.concatenate([jnp.where(m0, y, zero), jnp.where(m0, zero, y)], axis=0)


def _bdmm(x, y, m0, mm):
    return mm(x, _bd(y, m0), NN)


def _bdmm_nt(x, y, m0, mm):
    return mm(x, _bd(y, m0), NT)


def _bdmm_tn(x, y, m0, mm):
    a = mm(x, y, TN)
    return jnp.where(m0, a[0:HEAD_DIM, :], a[HEAD_DIM:PAIR, :])


def _map(f, *lists):
    return [f(*xs) for xs in zip(*lists)]


def _chunk_pairs(r, cum, lw, k, v, kk, b):
    L = CHUNK
    m0 = _pair_masks((L, PAIR))
    trow = lax.broadcasted_iota(jnp.int32, (L, PAIR), 0)
    icol = lax.broadcasted_iota(jnp.int32, (L, PAIR), 1) % HEAD_DIM
    strict = icol < trow
    incl = icol <= trow
    eye = (icol == trow).astype(F32)
    zero = jnp.zeros((L, PAIR), F32)

    cl = [c[L - 1:L, :] for c in cum]
    alpha = _map(lambda x, c, w: x * jnp.exp(c - w), kk, cum, lw)
    rho = _map(lambda x, c: x * jnp.exp(c), r, cum)
    einv = [jnp.exp(-c) for c in cum]
    kappa = _map(lambda x, e: x * e, k, einv)
    beta = _map(lambda x, e: x * e, b, einv)
    etail = _map(lambda c1, c: jnp.exp(c1 - c), cl, cum)
    kappa2 = _map(lambda x, e: x * e, k, etail)
    beta2 = _map(lambda x, e: x * e, b, etail)
    dl = [jnp.exp(c1) for c1 in cl]

    mm = _mm1

    def rows2(x0, x1, y, dims):
        out = mm(jnp.concatenate([x0, x1], axis=0), _bd(y, m0), dims)
        return out[0:L, :], out[L:2 * L, :]

    sk = _map(lambda a, q, y: rows2(a, q, y, NT), alpha, rho, kappa)
    sb = _map(lambda a, q, y: rows2(a, q, y, NT), alpha, rho, beta)
    m_k = [jnp.where(strict, x[0], zero) for x in sk]
    n_k = [jnp.where(incl, x[1], zero) for x in sk]
    m_b = [jnp.where(strict, x[0], zero) for x in sb]
    n_b = [jnp.where(incl, x[1], zero) for x in sb]

    nn = [-m for m in m_b]
    tinv = [eye + q for q in nn]
    pw = _map(lambda q: _bdmm(q, q, m0, mm), nn)
    mvn = _map(lambda a, q, y: rows2(a, q, y, NN), m_k, n_k, v)
    for _ in range(4):
        res = _map(lambda q, t: rows2(q, t, q, NN), pw, tinv)
        tinv = _map(lambda t, x: t + x[1], tinv, res)
        pw = [x[0] for x in res]
    tinv = _map(lambda t, q: t + _bdmm(t, q, m0, mm), tinv, pw)

    mv = [x[0] for x in mvn]
    nkv = [x[1] for x in mvn]
    alpha2 = _map(lambda t, x: _bdmm(t, x, m0, mm), tinv, alpha)
    w = _map(lambda t, x: _bdmm(t, x, m0, mm), tinv, mv)
    rho2 = _map(lambda x, n, a2: x - _bdmm(n, a2, m0, mm), rho, n_b, alpha2)
    o2 = _map(lambda x, n, y: x - _bdmm(n, y, m0, mm), nkv, n_b, w)

    def tn2(x0, x1, y):
        a = mm(jnp.concatenate([x0, x1], axis=1), y, TN)
        pick = lambda z: jnp.where(m0, z[0:HEAD_DIM, :], z[HEAD_DIM:PAIR, :])
        return pick(a[0:PAIR, :]), pick(a[PAIR:2 * PAIR, :])

    ab = _map(tn2, alpha2, w, beta2)
    g = _map(lambda d, x: eye * d - x[0], dl, ab)
    h = _map(lambda x, k2, y: _bdmm_tn(x, k2, m0, mm) - y[1], v, kappa2, ab)
    return rho2, o2, g, h


def _wkv_pre_kernel(r_ref, lw_ref, k_ref, v_ref, kk_ref, b_ref, rho_ref, o_ref, g_ref, h_ref):
    L = CHUNK
    rows, width = r_ref.shape[1], r_ref.shape[2]
    ti = lax.broadcasted_iota(jnp.int32, (L, L), 0)
    tj = lax.broadcasted_iota(jnp.int32, (L, L), 1)
    ltri = (tj <= ti).astype(BF16)
    items = [(pl.ds(j * L, L), slice(p * PAIR, (p + 1) * PAIR))
             for j in range(rows // L) for p in range(width // PAIR)]
    lw = [lw_ref[0, rs, sl] for rs, sl in items]
    cum = []
    for x in lw:
        h3, m3, l3 = _split3(x)
        cum.append(_dg(ltri, h3, NN) + (_dg(ltri, m3, NN) + _dg(ltri, l3, NN)))
    pick = lambda ref: [ref[0, rs, sl] for rs, sl in items]
    outs = _chunk_pairs(pick(r_ref), cum, lw, pick(k_ref), pick(v_ref), pick(kk_ref), pick(b_ref))
    for ref, vals in zip((rho_ref, o_ref, g_ref, h_ref), outs):
        for (rs, sl), val in zip(items, vals):
            ref[0, rs, sl] = val


def _wkv_pre(r, lw, k, v, kk, b, rows):
    B, T, D = r.shape
    blk = pl.BlockSpec((1, rows, D), lambda bi, c: (bi, c, 0))
    return pl.pallas_call(
        _wkv_pre_kernel,
        grid=(B, T // rows),
        in_specs=[blk] * 6,
        out_specs=[blk] * 4,
        out_shape=[jax.ShapeDtypeStruct((B, T, D), F32)] * 4,
        compiler_params=_cparams(("parallel", "parallel")),
        name="wkv_pre",
    )(r, lw, k, v, kk, b)


def _wkv_seq_kernel(rho_ref, o2_ref, g_ref, h_ref, o_ref, s_out_ref, s_ref):
    c = pl.program_id(1)
    nc = pl.num_programs(1)
    m0 = _pair_masks((CHUNK, PAIR))

    @pl.when(c == 0)
    def _():
        s_ref[...] = jnp.zeros_like(s_ref)

    sls = [slice(p * PAIR, (p + 1) * PAIR) for p in range(D_MODEL // PAIR)]
    s = [s_ref[:, sl] for sl in sls]
    for j in range(rho_ref.shape[1] // CHUNK):
        rs = pl.ds(j * CHUNK, CHUNK)
        o = [_bdmm_nt(rho_ref[0, rs, sl], x, m0, _mm1) + o2_ref[0, rs, sl] for sl, x in zip(sls, s)]
        s = [_bdmm(x, g_ref[0, rs, sl], m0, _mm3) + h_ref[0, rs, sl] for sl, x in zip(sls, s)]
        for sl, ov in zip(sls, o):
            o_ref[0, rs, sl] = ov
    for sl, sv in zip(sls, s):
        s_ref[:, sl] = sv

    @pl.when(c == nc - 1)
    def _():
        for hd in range(N_HEADS):
            s_out_ref[0, hd] = s_ref[:, hd * HEAD_DIM:(hd + 1) * HEAD_DIM]


def _wkv_seq(rho2, o2, g, h, rows):
    B, T, D = rho2.shape
    blk = pl.BlockSpec((1, rows, D), lambda bi, c: (bi, c, 0))
    return pl.pallas_call(
        _wkv_seq_kernel,
        grid=(B, T // rows),
        in_specs=[blk] * 4,
        out_specs=[blk, pl.BlockSpec((1, N_HEADS, HEAD_DIM, HEAD_DIM), lambda bi, c: (bi, 0, 0, 0))],
        out_shape=[jax.ShapeDtypeStruct((B, T, D), F32),
                   jax.ShapeDtypeStruct((B, N_HEADS, HEAD_DIM, HEAD_DIM), F32)],
        scratch_shapes=[pltpu.VMEM((HEAD_DIM, D), F32)],
        compiler_params=_cparams(("parallel", "arbitrary")),
        name="wkv_seq",
    )(rho2, o2, g, h)


def _wkv_step_kernel(s_ref, vec_ref, o_ref, so_ref):
    r, lw, k, v, kk, b = [vec_ref[i] for i in range(6)]
    d = jnp.exp(lw)
    for vi in range(HEAD_DIM):
        s = s_ref[0, vi]
        sa = -jnp.sum(s * kk, axis=0, keepdims=True)
        s_new = s * d + sa * b + v[vi:vi + 1, :] * k
        so_ref[0, vi] = s_new
        o_ref[pl.ds(vi, 1), :] = jnp.sum(s_new * r, axis=0, keepdims=True)


def _wkv_step(s_t, vecs_t):
    n = s_t.shape[-1]
    st = pl.BlockSpec((1, HEAD_DIM, HEAD_DIM, n), lambda h: (h, 0, 0, 0))
    return pl.pallas_call(
        _wkv_step_kernel,
        grid=(N_HEADS,),
        in_specs=[st, pl.BlockSpec((6, HEAD_DIM, n), lambda h: (0, h, 0))],
        out_specs=[pl.BlockSpec((HEAD_DIM, n), lambda h: (h, 0)), st],
        out_shape=[jax.ShapeDtypeStruct((D_MODEL, n), F32), jax.ShapeDtypeStruct(s_t.shape, F32)],
        compiler_params=_cparams(("parallel",)),
        name="wkv_step",
    )(s_t, vecs_t)


def _first_max(x, axis, n):
    m = jnp.max(x, axis=axis, keepdims=True)
    idx = lax.broadcasted_iota(jnp.int32, x.shape, axis)
    first = jnp.min(jnp.where(x == m, idx, n), axis=axis, keepdims=True)
    return m, idx == first


def _route(scores, biased):
    tm = scores.shape[1]
    per = N_EXPERTS // N_GROUPS
    neg = jnp.full((), -jnp.inf, F32)
    b3 = biased.reshape(N_GROUPS, per, tm)
    m1, hit = _first_max(b3, 1, per)
    m2 = jnp.max(jnp.where(hit, neg, b3), axis=1, keepdims=True)
    gs = (m1 + m2).reshape(N_GROUPS, tm)
    gsel = jnp.zeros((N_GROUPS, tm), jnp.bool_)
    for _ in range(TOPK_GROUPS):
        _, hit = _first_max(gs, 0, N_GROUPS)
        gsel = jnp.logical_or(gsel, hit)
        gs = jnp.where(hit, neg, gs)
    emask = jnp.broadcast_to(gsel.reshape(N_GROUPS, 1, tm), (N_GROUPS, per, tm)).reshape(N_EXPERTS, tm)
    cand = jnp.where(emask, biased, neg)
    esel = jnp.zeros((N_EXPERTS, tm), jnp.bool_)
    for _ in range(TOP_K):
        _, hit = _first_max(cand, 0, N_EXPERTS)
        esel = jnp.logical_or(esel, hit)
        cand = jnp.where(hit, neg, cand)
    wsel = jnp.where(esel, scores, 0.0)
    return wsel / jnp.sum(wsel, axis=0, keepdims=True) * ROUTED_SCALE


def _post_kernel(o_ref, g_ref, bo_ref, ga_ref, bm_ref, x_ref,
                 gng_ref, gnb_ref, wo_ref, n2_ref, wsg_ref, wsu_ref, wsd_ref, wrt_ref, eb_ref,
                 base_ref, hn_ref, gate_ref):
    seg, exp = _seg_mats()
    o = o_ref[...]
    inv_n = 1.0 / HEAD_DIM
    mean = _mm_exact_rhs(_mm_exact_rhs(o, seg, passes=2) * inv_n, exp, passes=2)
    d = o - mean
    var = _mm_exact_rhs(d * d, seg, passes=1) * inv_n
    rstd = _mm_exact_rhs(lax.rsqrt(var + GN_EPS), exp, passes=1)
    ogn = d * rstd * gng_ref[...] + gnb_ref[...]
    out_a = (ogn + bo_ref[...]) * g_ref[...]
    merged = ga_ref[...] * out_a + bm_ref[...]
    h = x_ref[...] + _dg(merged.astype(BF16), wo_ref[...], NN)
    hn = _rmsnorm(h, n2_ref[...])
    hb = hn.astype(BF16)
    sg = _dg(hb, wsg_ref[...], NN)
    su = _dg(hb, wsu_ref[...], NN)
    shared = _dg((sg * _sigmoid(sg) * su).astype(BF16), wsd_ref[...], NN)
    base_ref[...] = h + shared
    hn_ref[...] = hb
    logits = _mm3(wrt_ref[...], hn, NT)
    scores = _sigmoid(logits)
    gate_t = _route(scores, scores + eb_ref[...])
    tm = gate_t.shape[1]
    gate_pad = jnp.concatenate([gate_t, jnp.zeros((LANES - N_EXPERTS, tm), F32)], axis=0)
    gate_ref[...] = gate_pad.T


def _post(o, g, bonus, ga, bm, x, p, tm):
    N, D = x.shape
    tm = min(tm, N)
    params = (p['gn_g'], p['gn_b'], p['w_out'], p['norm2_g'], p['ws_gate'], p['ws_up'], p['ws_down'],
              p['w_router_t'], p['e_bias'])
    row = pl.BlockSpec((tm, D), lambda i: (i, 0))
    return pl.pallas_call(
        _post_kernel,
        grid=(N // tm,),
        in_specs=[row] * 6 + [_full(a.shape) for a in params],
        out_specs=[row, row, pl.BlockSpec((tm, LANES), lambda i: (i, 0))],
        out_shape=[jax.ShapeDtypeStruct((N, D), F32), jax.ShapeDtypeStruct((N, D), BF16),
                   jax.ShapeDtypeStruct((N, LANES), F32)],
        compiler_params=_cparams(("parallel",)),
        name="post",
    )(o, g, bonus, ga, bm, x, *params)


def _moe_kernel(x_ref, gate_ref, base_ref, wg_ref, wu_ref, wd_ref, nf_ref, y_ref):
    j = pl.program_id(1)
    nj = pl.num_programs(1)
    eps = wg_ref.shape[0]
    gate = gate_ref[...]
    lane = lax.broadcasted_iota(jnp.int32, gate.shape, 1)
    x = x_ref[...]
    cols = [jnp.sum(jnp.where(lane == j * eps + q, gate, 0.0), axis=1, keepdims=True) for q in range(eps)]
    hg = [_dg(x, wg_ref[q], NN) for q in range(eps)]
    hu = [_dg(x, wu_ref[q], NN) for q in range(eps)]
    hh = [(hg[q] * _sigmoid(hg[q]) * hu[q] * cols[q]).astype(BF16) for q in range(eps)]
    down = _dg(jnp.concatenate(hh, axis=1), wd_ref[...].reshape(eps * D_EXPERT, D_MODEL), NN)

    @pl.when(j == 0)
    def _():
        y_ref[...] = base_ref[...] + down

    @pl.when(j > 0)
    def _():
        y_ref[...] += down

    @pl.when(j == nj - 1)
    def _():
        y_ref[...] = _rmsnorm(y_ref[...], nf_ref[...])


def _moe(hn, gate, base, p, tm, eps):
    N, D = base.shape
    tm = min(tm, N)
    row = lambda w: pl.BlockSpec((tm, w), lambda i, e: (i, 0))
    return pl.pallas_call(
        _moe_kernel,
        grid=(N // tm, N_EXPERTS // eps),
        in_specs=[row(D), row(LANES), row(D),
                  pl.BlockSpec((eps, D, D_EXPERT), lambda i, e: (e, 0, 0)),
                  pl.BlockSpec((eps, D, D_EXPERT), lambda i, e: (e, 0, 0)),
                  pl.BlockSpec((eps, D_EXPERT, D), lambda i, e: (e, 0, 0)),
                  pl.BlockSpec((1, D), lambda i, e: (0, 0))],
        out_specs=row(D),
        out_shape=jax.ShapeDtypeStruct((N, D), F32),
        compiler_params=_cparams(("parallel", "arbitrary")),
        name="moe",
    )(hn, gate, base, p['w_gate'], p['w_up'], p['w_down'], p['normf_g'])


def kernel(x_prompt, x_sample, state_wkv, state_shift, state_conv, norm1_g, w_in, mu_shift, w0, w_decay_up, a0, a_up, g_up, k_k, k_a, r_k, gn_g, gn_b, conv_w, conv_b, cln_g, cln_b, w_out, norm2_g, w_router, e_bias, w_gate, w_up, w_down, ws_gate, ws_up, ws_down, normf_g):
    depth = w_in.shape[0]
    assert depth == 1
    B, T, D = x_prompt.shape
    NS = x_sample.shape[0]
    assert x_sample.shape[1] == 1 and D == D_MODEL and T % CHUNK == 0
    row = lambda a: a[0].reshape(1, -1)
    p = {
        'norm1_g': row(norm1_g), 'mu_shift': row(mu_shift), 'w0': row(w0), 'a0': row(a0),
        'k_k': row(k_k), 'k_a': row(k_a), 'r_k': row(r_k), 'gn_g': row(gn_g), 'gn_b': row(gn_b),
        'conv_b': row(conv_b), 'cln_g': row(cln_g), 'cln_b': row(cln_b), 'norm2_g': row(norm2_g),
        'normf_g': normf_g.reshape(1, -1),
        'w_rwkv': w_in[0, :, :C_RWKV].astype(BF16), 'w_rest': w_in[0, :, C_RWKV:].astype(BF16),
        'w_decay_up': w_decay_up[0], 'a_up': a_up[0], 'g_up': g_up[0], 'conv_w': conv_w[0],
        'w_out': w_out[0].astype(BF16),
        'ws_gate': ws_gate[0].astype(BF16), 'ws_up': ws_up[0].astype(BF16), 'ws_down': ws_down[0].astype(BF16),
        'w_router_t': w_router[0].T, 'e_bias': e_bias[0].reshape(-1, 1),
        'w_gate': w_gate[0].astype(BF16), 'w_up': w_up[0].astype(BF16), 'w_down': w_down[0].astype(BF16),
    }

    zp0 = jnp.zeros((B, 1, C_RWKV), F32)
    r, lw, k, v, kk, b, g, bonus, shift_p, ga, bm, conv_p = _mix_seq(
        x_prompt, zp0, jnp.zeros((B, CONV_WIDTH - 1, D), F32), p, tm=256)
    rho2, o2, gm, hm = _wkv_pre(r, lw, k, v, kk, b, rows=2 * CHUNK)
    o, wkv_p = _wkv_seq(rho2, o2, gm, hm, rows=4 * CHUNK)
    flat = lambda a: a.reshape(B * T, D)
    base, hn, gate = _post(flat(o), flat(g), flat(bonus), flat(ga), flat(bm), flat(x_prompt), p, tm=512)
    y_prompt = _moe(hn, gate, base, p, tm=1024, eps=4).reshape(B, T, D)

    xs = x_sample.reshape(NS, D)
    r, lw, k, v, kk, b, g, bonus, shift_s = _rwkv_prep_batch(xs, state_shift[0], p)
    ga, bm, conv_s_t = _conv_branch_batch(xs, jnp.swapaxes(state_conv[0], 0, 1), p)
    o_t, wkv_s_t = _wkv_step(jnp.transpose(state_wkv[0], (1, 2, 3, 0)),
                             jnp.transpose(jnp.stack([r, lw, k, v, kk, b]), (0, 2, 1)))
    o = o_t.T
    wkv_s = jnp.transpose(wkv_s_t, (3, 0, 1, 2))
    base, hn, gate = _post(o, g, bonus, ga, bm, xs, p, tm=128)
    y_sample = _moe(hn, gate, base, p, tm=128, eps=4).reshape(NS, 1, D)

    return (y_prompt, y_sample, wkv_p[None], shift_p.reshape(1, B, D), conv_p[None],
            wkv_s[None], shift_s[None], jnp.swapaxes(conv_s_t, 0, 1)[None])
```

```python
import functools

import jax
import jax.numpy as jnp
from jax import lax
from jax.experimental import pallas as pl
from jax.experimental.pallas import tpu as pltpu

F32 = jnp.float32
BF16 = jnp.bfloat16

D_MODEL = 1024
HEAD_DIM = 64
N_HEADS = D_MODEL // HEAD_DIM
D_DECAY_LORA = 64
D_AAA_LORA = 64
D_GATE_LORA = 128
GN_EPS = 64e-5
CONV_WIDTH = 31
LN_EPS = 1e-5
N_EXPERTS = 64
N_GROUPS = 8
TOPK_GROUPS = 4
TOP_K = 8
D_EXPERT = 256
ROUTED_SCALE = 2.5
RMS_EPS = 1e-6

O_K = D_MODEL
O_V = 2 * D_MODEL
O_W = 3 * D_MODEL
O_A = O_W + D_DECAY_LORA
O_G = O_A + D_AAA_LORA
C_RWKV = O_G + D_GATE_LORA
C_REST = 4 * D_MODEL

LANES = 128
SUBLANES = 8
CHUNK = 64
PAIR = 2 * HEAD_DIM
VMEM_LIMIT = 56 * 1024 * 1024

NN = ((1,), (0,))
NT = ((1,), (1,))
TN = ((0,), (0,))


def _dg(a, b, dims):
    return lax.dot_general(a, b, (dims, ((), ())), preferred_element_type=F32)


def _split2(x):
    hi = x.astype(BF16)
    lo = (x - hi.astype(F32)).astype(BF16)
    return hi, lo


def _split3(x):
    hi = x.astype(BF16)
    r1 = x - hi.astype(F32)
    mid = r1.astype(BF16)
    lo = (r1 - mid.astype(F32)).astype(BF16)
    return hi, mid, lo


def _mm1(a, b, dims=NN):
    return _dg(a.astype(BF16), b.astype(BF16), dims)


def _mm3(a, b, dims=NN):
    ah, al = _split2(a)
    bh, bl = _split2(b)
    return _dg(ah, bh, dims) + (_dg(ah, bl, dims) + _dg(al, bh, dims))


def _mm_exact_rhs(a, b_bf16, dims=NN, passes=3):
    if passes == 1:
        return _dg(a.astype(BF16), b_bf16, dims)
    if passes == 2:
        h, l = _split2(a)
        return _dg(h, b_bf16, dims) + _dg(l, b_bf16, dims)
    h, m, l = _split3(a)
    return _dg(h, b_bf16, dims) + (_dg(m, b_bf16, dims) + _dg(l, b_bf16, dims))


def _rmsnorm(x, g):
    return x * lax.rsqrt(jnp.mean(x * x, axis=-1, keepdims=True) + RMS_EPS) * g


def _sigmoid(x):
    return 1.0 / (1.0 + jnp.exp(-x))


def _seg_mats():
    row = lax.broadcasted_iota(jnp.int32, (D_MODEL, LANES), 0) // HEAD_DIM
    col = lax.broadcasted_iota(jnp.int32, (D_MODEL, LANES), 1)
    seg = (row == col).astype(BF16)
    rowt = lax.broadcasted_iota(jnp.int32, (LANES, D_MODEL), 0)
    colt = lax.broadcasted_iota(jnp.int32, (LANES, D_MODEL), 1) // HEAD_DIM
    exp = (rowt == colt).astype(BF16)
    return seg, exp


def _cparams(sem):
    return pltpu.CompilerParams(dimension_semantics=sem, vmem_limit_bytes=VMEM_LIMIT)


def _full(shape):
    n = len(shape)
    return pl.BlockSpec(shape, lambda *_: (0,) * n)


def _prep_math(zr, zp, mu, w0, wdu, a0, aup, gup, kk_w, ka_w, rk_w, seg, exp):
    zm = zr + (zp - zr) * mu
    r = zm[:, 0:O_K]
    k = zm[:, O_K:O_V]
    v = zm[:, O_V:O_W]
    xw = jnp.tanh(zm[:, O_W:O_A])
    xa = zm[:, O_A:O_G]
    xg = _sigmoid(zm[:, O_G:C_RWKV])
    y = -(w0 + _mm1(xw, wdu))
    softplus = jnp.maximum(y, 0.0) + jnp.log(1.0 + jnp.exp(-jnp.abs(y)))
    lw = -jnp.exp(-softplus - 0.5)
    a = _sigmoid(a0 + _mm1(xa, aup))
    g = _mm1(xg, gup)
    kkr = k * kk_w
    ss = _mm_exact_rhs(kkr * kkr, seg, passes=2)
    inv = 1.0 / jnp.maximum(jnp.sqrt(ss), 1e-12)
    kk = kkr * _mm_exact_rhs(inv, exp, passes=2)
    kf = k * (1.0 + (a - 1.0) * ka_w)
    b = kk * a
    rk = _mm_exact_rhs(r * kf * rk_w, seg, passes=1)
    bonus = _mm_exact_rhs(rk, exp, passes=1) * v
    return r, lw, kf, v, kk, b, g, bonus


def _prep_seq_body(zr, vec_refs, carry_ref):
    tm = zr.shape[0]
    rows = lax.broadcasted_iota(jnp.int32, zr.shape, 0)
    zp = jnp.where(rows == 0, carry_ref[...], pltpu.roll(zr, 1, 0))
    carry_ref[...] = zr[tm - 1:tm, :]
    seg, exp = _seg_mats()
    return _prep_math(zr, zp, *[ref[...] for ref in vec_refs], seg, exp)


def _prep_batch_kernel(x_ref, xp_ref, g1_ref, w_ref, mu_ref, w0_ref, wdu_ref, a0_ref, aup_ref, gup_ref,
                       kkw_ref, kaw_ref, rkw_ref,
                       r_ref, lw_ref, k_ref, v_ref, kk_ref, b_ref, g_ref, bo_ref, xn_ref):
    xn = _rmsnorm(x_ref[...], g1_ref[...])
    w = w_ref[...]
    zr = _dg(xn.astype(BF16), w, NN)
    zp = _dg(xp_ref[...].astype(BF16), w, NN)
    seg, exp = _seg_mats()
    outs = _prep_math(zr, zp, mu_ref[...], w0_ref[...], wdu_ref[...], a0_ref[...], aup_ref[...], gup_ref[...],
                      kkw_ref[...], kaw_ref[...], rkw_ref[...], seg, exp)
    for o_ref, val in zip((r_ref, lw_ref, k_ref, v_ref, kk_ref, b_ref, g_ref, bo_ref), outs):
        o_ref[...] = val
    xn_ref[...] = xn


def _prep_params(p):
    return (p['norm1_g'], p['w_rwkv'], p['mu_shift'], p['w0'], p['w_decay_up'], p['a0'], p['a_up'], p['g_up'],
            p['k_k'], p['k_a'], p['r_k'])


def _rwkv_prep_batch(x, xprev, p):
    N, D = x.shape
    params = _prep_params(p)
    out_shape = [jax.ShapeDtypeStruct((N, D), F32)] * 9
    return pl.pallas_call(
        _prep_batch_kernel,
        grid=(1,),
        in_specs=[_full(x.shape), _full(xprev.shape)] + [_full(a.shape) for a in params],
        out_specs=[_full((N, D))] * 9,
        out_shape=out_shape,
        compiler_params=_cparams(("arbitrary",)),
        name="rwkv_prep_batch",
    )(x, xprev, *params)


HIST = 32


def _conv_tail(c, cb, lg, lb, mix_a, mix_b):
    c = c + cb
    mean = jnp.mean(c, axis=-1, keepdims=True)
    d = c - mean
    var = jnp.mean(d * d, axis=-1, keepdims=True)
    y = d * lax.rsqrt(var + LN_EPS) * lg + lb
    out_b = y * _sigmoid(y)
    return _sigmoid(mix_a), _sigmoid(mix_b) * out_b


def _glu_mix(xn, w_ref):
    xb = xn.astype(BF16)
    glu_a = _dg(xb, w_ref[:, 0:D_MODEL], NN)
    glu_b = _dg(xb, w_ref[:, D_MODEL:2 * D_MODEL], NN)
    mix_a = _dg(xb, w_ref[:, 2 * D_MODEL:3 * D_MODEL], NN)
    mix_b = _dg(xb, w_ref[:, 3 * D_MODEL:4 * D_MODEL], NN)
    return glu_a * _sigmoid(glu_b), mix_a, mix_b


def _conv_taps(u, cw_ref, ubuf_ref):
    tm = u.shape[0]
    npast = CONV_WIDTH - 1
    ubuf_ref[pl.ds(HIST, tm), :] = u
    c = cw_ref[pl.ds(npast, 1), :] * ubuf_ref[pl.ds(HIST, tm), :]
    for s in range(SUBLANES):
        offs = [o for o in range(HIST - npast, HIST) if o % SUBLANES == s]
        grp = None
        for o in offs:
            term = cw_ref[pl.ds(o - (HIST - npast), 1), :] * ubuf_ref[pl.ds(o - s, tm + SUBLANES), :]
            grp = term if grp is None else grp + term
        c = c + grp[s:s + tm, :]
    return c


def _mix_seq_kernel(x_ref, zp0_ref, sc_ref, g1_ref, wr_ref, mu_ref, w0_ref, wdu_ref, a0_ref, aup_ref, gup_ref,
                    kkw_ref, kaw_ref, rkw_ref, wc_ref, cw_ref, cb_ref, lg_ref, lb_ref,
                    r_ref, lw_ref, k_ref, v_ref, kk_ref, b_ref, g_ref, bo_ref, xl_ref, ga_ref, bm_ref, so_ref,
                    carry_ref, ubuf_ref):
    t = pl.program_id(1)
    nt = pl.num_programs(1)
    tm = x_ref.shape[1]
    npast = CONV_WIDTH - 1

    @pl.when(t == 0)
    def _():
        carry_ref[...] = zp0_ref[0]
        ubuf_ref[pl.ds(0, HIST - npast), :] = jnp.zeros((HIST - npast, D_MODEL), F32)
        ubuf_ref[pl.ds(HIST - npast, npast), :] = sc_ref[0]

    @pl.when(t > 0)
    def _():
        ubuf_ref[pl.ds(0, HIST), :] = ubuf_ref[pl.ds(tm, HIST), :]

    xn = _rmsnorm(x_ref[0], g1_ref[...])
    xb = xn.astype(BF16)
    glu_a = _dg(xb, wc_ref[:, 0:D_MODEL], NN)
    glu_b = _dg(xb, wc_ref[:, D_MODEL:2 * D_MODEL], NN)
    c = _conv_taps(glu_a * _sigmoid(glu_b), cw_ref, ubuf_ref)
    zr = _dg(xb, wr_ref[...], NN)
    mix_a = _dg(xb, wc_ref[:, 2 * D_MODEL:3 * D_MODEL], NN)
    mix_b = _dg(xb, wc_ref[:, 3 * D_MODEL:4 * D_MODEL], NN)
    ga, bm = _conv_tail(c, cb_ref[...], lg_ref[...], lb_ref[...], mix_a, mix_b)
    outs = _prep_seq_body(zr, (mu_ref, w0_ref, wdu_ref, a0_ref, aup_ref, gup_ref, kkw_ref, kaw_ref, rkw_ref), carry_ref)
    for o_ref, val in zip((r_ref, lw_ref, k_ref, v_ref, kk_ref, b_ref, g_ref, bo_ref), outs):
        o_ref[0] = val
    xl_ref[0] = xn[tm - 1:tm, :]
    ga_ref[0] = ga
    bm_ref[0] = bm

    @pl.when(t == nt - 1)
    def _():
        so_ref[0] = ubuf_ref[pl.ds(tm + HIST - npast, npast), :]


def _conv_batch_kernel(x_ref, sc_ref, g1_ref, w_ref, cw_ref, cb_ref, lg_ref, lb_ref,
                       ga_ref, bm_ref, so_ref):
    npast = CONV_WIDTH - 1
    xn = _rmsnorm(x_ref[...], g1_ref[...])
    u, mix_a, mix_b = _glu_mix(xn, w_ref)
    c = cw_ref[pl.ds(npast, 1), :] * u
    for j in range(npast):
        c = c + cw_ref[pl.ds(j, 1), :] * sc_ref[j]
    ga, bm = _conv_tail(c, cb_ref[...], lg_ref[...], lb_ref[...], mix_a, mix_b)
    ga_ref[...] = ga
    bm_ref[...] = bm
    for j in range(npast - 1):
        so_ref[j] = sc_ref[j + 1]
    so_ref[npast - 1] = u


def _conv_params(p):
    return (p['norm1_g'], p['w_rest'], p['conv_w'], p['conv_b'], p['cln_g'], p['cln_b'])


def _resident(shape):
    n = len(shape)
    return pl.BlockSpec(shape, lambda *_: (0,) * n, pipeline_mode=pl.Buffered(1))


def _mix_seq(x, zp0, s_conv, p, tm):
    B, T, D = x.shape
    tm = min(tm, T)
    params = _prep_params(p) + _conv_params(p)[1:]
    npast = CONV_WIDTH - 1
    seq = pl.BlockSpec((1, tm, D), lambda b, t: (b, t, 0))
    per_b = lambda n: pl.BlockSpec((1, n, D), lambda b, t: (b, 0, 0))
    return pl.pallas_call(
        _mix_seq_kernel,
        grid=(B, T // tm),
        in_specs=[seq, pl.BlockSpec((1, 1, C_RWKV), lambda b, t: (b, 0, 0)), per_b(npast)]
                 + [_resident(a.shape) for a in params],
        out_specs=[seq] * 8 + [per_b(1), seq, seq, per_b(npast)],
        out_shape=[jax.ShapeDtypeStruct((B, T, D), F32)] * 8 + [jax.ShapeDtypeStruct((B, 1, D), F32)]
                  + [jax.ShapeDtypeStruct((B, T, D), F32)] * 2 + [jax.ShapeDtypeStruct((B, npast, D), F32)],
        scratch_shapes=[pltpu.VMEM((1, C_RWKV), F32), pltpu.VMEM((tm + HIST, D), F32)],
        compiler_params=_cparams(("parallel", "arbitrary")),
        name="mix_seq",
    )(x, zp0, s_conv, *params)


def _conv_branch_batch(x, s_conv_t, p):
    N, D = x.shape
    params = _conv_params(p)
    return pl.pallas_call(
        _conv_batch_kernel,
        grid=(1,),
        in_specs=[_full(x.shape), _full(s_conv_t.shape)] + [_full(a.shape) for a in params],
        out_specs=[_full((N, D)), _full((N, D)), _full(s_conv_t.shape)],
        out_shape=[jax.ShapeDtypeStruct((N, D), F32)] * 2 + [jax.ShapeDtypeStruct(s_conv_t.shape, F32)],
        compiler_params=_cparams(("arbitrary",)),
        name="conv_branch_batch",
    )(x, s_conv_t, *params)


def _pair_masks(shape):
    lane = lax.broadcasted_iota(jnp.int32, shape, 1)
    return lane < HEAD_DIM


def _bd(y, m0):
    zero = jnp.zeros_like(y)
    return jnp.concatenate([jnp.where(m0, y, zero), jnp.where(m0, zero, y)], axis=0)


def _bdmm(x, y, m0, mm):
    return mm(x, _bd(y, m0), NN)


def _bdmm_nt(x, y, m0, mm):
    return mm(x, _bd(y, m0), NT)


def _bdmm_tn(x, y, m0, mm):
    a = mm(x, y, TN)
    return jnp.where(m0, a[0:HEAD_DIM, :], a[HEAD_DIM:PAIR, :])


def _map(f, *lists):
    return [f(*xs) for xs in zip(*lists)]


def _chunk_pairs(r, cum, lw, k, v, kk, b, hooks=()):
    L = CHUNK
    hooks = list(hooks)

    def run_hook():
        if hooks:
            hooks.pop(0)()

    m0 = _pair_masks((L, PAIR))
    trow = lax.broadcasted_iota(jnp.int32, (L, PAIR), 0)
    icol = lax.broadcasted_iota(jnp.int32, (L, PAIR), 1) % HEAD_DIM
    strict = icol < trow
    incl = icol <= trow
    eye = (icol == trow).astype(F32)
    zero = jnp.zeros((L, PAIR), F32)

    cl = [c[L - 1:L, :] for c in cum]
    alpha = _map(lambda x, c, w: x * jnp.exp(c - w), kk, cum, lw)
    rho = _map(lambda x, c: x * jnp.exp(c), r, cum)
    einv = [jnp.exp(-c) for c in cum]
    kappa = _map(lambda x, e: x * e, k, einv)
    beta = _map(lambda x, e: x * e, b, einv)
    etail = _map(lambda c1, c: jnp.exp(c1 - c), cl, cum)
    kappa2 = _map(lambda x, e: x * e, k, etail)
    beta2 = _map(lambda x, e: x * e, b, etail)
    dl = [jnp.exp(c1) for c1 in cl]

    mm = _mm1

    def rows2(x0, x1, y, dims):
        out = mm(jnp.concatenate([x0, x1], axis=0), _bd(y, m0), dims)
        return out[0:L, :], out[L:2 * L, :]

    sk = _map(lambda a, q, y: rows2(a, q, y, NT), alpha, rho, kappa)
    sb = _map(lambda a, q, y: rows2(a, q, y, NT), alpha, rho, beta)
    m_k = [jnp.where(strict, x[0], zero) for x in sk]
    n_k = [jnp.where(incl, x[1], zero) for x in sk]
    m_b = [jnp.where(strict, x[0], zero) for x in sb]
    n_b = [jnp.where(incl, x[1], zero) for x in sb]
    run_hook()

    nn = [-m for m in m_b]
    tinv = [eye + q for q in nn]
    pw = _map(lambda q: _bdmm(q, q, m0, mm), nn)
    mvn = _map(lambda a, q, y: rows2(a, q, y, NN), m_k, n_k, v)
    for it in range(4):
        res = _map(lambda q, t: rows2(q, t, q, NN), pw, tinv)
        tinv = _map(lambda t, x: t + x[1], tinv, res)
        pw = [x[0] for x in res]
        if it % 2 == 1:
            run_hook()
    tinv = _map(lambda t, q: t + _bdmm(t, q, m0, mm), tinv, pw)

    mv = [x[0] for x in mvn]
    nkv = [x[1] for x in mvn]
    alpha2 = _map(lambda t, x: _bdmm(t, x, m0, mm), tinv, alpha)
    w = _map(lambda t, x: _bdmm(t, x, m0, mm), tinv, mv)
    run_hook()
    rho2 = _map(lambda x, n, a2: x - _bdmm(n, a2, m0, mm), rho, n_b, alpha2)
    o2 = _map(lambda x, n, y: x - _bdmm(n, y, m0, mm), nkv, n_b, w)

    def tn2(x0, x1, y):
        a = mm(jnp.concatenate([x0, x1], axis=1), y, TN)
        pick = lambda z: jnp.where(m0, z[0:HEAD_DIM, :], z[HEAD_DIM:PAIR, :])
        return pick(a[0:PAIR, :]), pick(a[PAIR:2 * PAIR, :])

    ab = _map(tn2, alpha2, w, beta2)
    g = _map(lambda d, x: eye * d - x[0], dl, ab)
    h = _map(lambda x, k2, y: _bdmm_tn(x, k2, m0, mm) - y[1], v, kappa2, ab)
    while hooks:
        run_hook()
    return rho2, o2, g, h


def _wkv_kernel(r_ref, lw_ref, k_ref, v_ref, kk_ref, b_ref, o_ref, s_out_ref, rho_s, o2_s, g_s, h_s, s_ref):
    j = pl.program_id(1)
    n = pl.num_programs(1)
    L = CHUNK
    rows, width = r_ref.shape[1], r_ref.shape[2]
    nsub = rows // L
    m0 = _pair_masks((L, PAIR))
    sls = [slice(p * PAIR, (p + 1) * PAIR) for p in range(width // PAIR)]

    @pl.when(j == 0)
    def _():
        for ref in (rho_s, o2_s, g_s, h_s, s_ref):
            ref[...] = jnp.zeros_like(ref)

    state = [s_ref[:, sl] for sl in sls]

    def recur(c, row0):
        rs = pl.ds(c * L, L)
        o = [_bdmm_nt(rho_s[rs, sl], x, m0, _mm1) + o2_s[rs, sl] for sl, x in zip(sls, state)]
        state[:] = [_bdmm(x, g_s[rs, sl], m0, _mm3) + h_s[rs, sl] for sl, x in zip(sls, state)]
        dst = pl.ds(pl.multiple_of(row0 + c * L, L), L)
        for sl, ov in zip(sls, o):
            o_ref[0, dst, sl] = ov

    prev_row0 = jnp.maximum(j - 1, 0) * rows
    hooks = [functools.partial(recur, c, prev_row0) for c in range(nsub)]

    ti = lax.broadcasted_iota(jnp.int32, (L, L), 0)
    tj = lax.broadcasted_iota(jnp.int32, (L, L), 1)
    ltri = (tj <= ti).astype(BF16)
    items = [(pl.ds(c * L, L), sl) for c in range(nsub) for sl in sls]
    lw = [lw_ref[0, rs, sl] for rs, sl in items]
    cum = []
    for x in lw:
        h3, m3, l3 = _split3(x)
        cum.append(_dg(ltri, h3, NN) + (_dg(ltri, m3, NN) + _dg(ltri, l3, NN)))
    pick = lambda ref: [ref[0, rs, sl] for rs, sl in items]
    outs = _chunk_pairs(pick(r_ref), cum, lw, pick(k_ref), pick(v_ref), pick(kk_ref), pick(b_ref), hooks)
    for ref, vals in zip((rho_s, o2_s, g_s, h_s), outs):
        for (rs, sl), val in zip(items, vals):
            ref[rs, sl] = val
    for sl, sv in zip(sls, state):
        s_ref[:, sl] = sv

    @pl.when(j == n - 1)
    def _():
        state[:] = [s_ref[:, sl] for sl in sls]
        for c in range(nsub):
            recur(c, j * rows)
        for p, sv in enumerate(state):
            s_out_ref[0, 2 * p] = sv[:, :HEAD_DIM]
            s_out_ref[0, 2 * p + 1] = sv[:, HEAD_DIM:]


def _wkv(r, lw, k, v, kk, b, rows):
    B, T, D = r.shape
    blk = pl.BlockSpec((1, rows, D), lambda bi, c: (bi, c, 0))
    return pl.pallas_call(
        _wkv_kernel,
        grid=(B, T // rows),
        in_specs=[blk] * 6,
        out_specs=[pl.BlockSpec((1, T, D), lambda bi, c: (bi, 0, 0)),
                   pl.BlockSpec((1, N_HEADS, HEAD_DIM, HEAD_DIM), lambda bi, c: (bi, 0, 0, 0))],
        out_shape=[jax.ShapeDtypeStruct((B, T, D), F32),
                   jax.ShapeDtypeStruct((B, N_HEADS, HEAD_DIM, HEAD_DIM), F32)],
        scratch_shapes=[pltpu.VMEM((rows, D), F32)] * 4 + [pltpu.VMEM((HEAD_DIM, D), F32)],
        compiler_params=_cparams(("parallel", "arbitrary")),
        name="wkv",
    )(r, lw, k, v, kk, b)


def _wkv_step_kernel(s_ref, vec_ref, o_ref, so_ref):
    r, lw, k, v, kk, b = [vec_ref[i] for i in range(6)]
    d = jnp.exp(lw)
    for vi in range(HEAD_DIM):
        s = s_ref[0, vi]
        sa = -jnp.sum(s * kk, axis=0, keepdims=True)
        s_new = s * d + sa * b + v[vi:vi + 1, :] * k
        so_ref[0, vi] = s_new
        o_ref[pl.ds(vi, 1), :] = jnp.sum(s_new * r, axis=0, keepdims=True)


def _wkv_step(s_t, vecs_t):
    n = s_t.shape[-1]
    st = pl.BlockSpec((1, HEAD_DIM, HEAD_DIM, n), lambda h: (h, 0, 0, 0))
    return pl.pallas_call(
        _wkv_step_kernel,
        grid=(N_HEADS,),
        in_specs=[st, pl.BlockSpec((6, HEAD_DIM, n), lambda h: (0, h, 0))],
        out_specs=[pl.BlockSpec((HEAD_DIM, n), lambda h: (h, 0)), st],
        out_shape=[jax.ShapeDtypeStruct((D_MODEL, n), F32), jax.ShapeDtypeStruct(s_t.shape, F32)],
        compiler_params=_cparams(("parallel",)),
        name="wkv_step",
    )(s_t, vecs_t)


def _first_max(x, axis, n):
    m = jnp.max(x, axis=axis, keepdims=True)
    idx = lax.broadcasted_iota(jnp.int32, x.shape, axis)
    first = jnp.min(jnp.where(x == m, idx, n), axis=axis, keepdims=True)
    return m, idx == first


def _route(scores, biased):
    tm = scores.shape[1]
    per = N_EXPERTS // N_GROUPS
    neg = jnp.full((), -jnp.inf, F32)
    b3 = biased.reshape(N_GROUPS, per, tm)
    m1, hit = _first_max(b3, 1, per)
    m2 = jnp.max(jnp.where(hit, neg, b3), axis=1, keepdims=True)
    gs = (m1 + m2).reshape(N_GROUPS, tm)
    gsel = jnp.zeros((N_GROUPS, tm), jnp.bool_)
    for _ in range(TOPK_GROUPS):
        _, hit = _first_max(gs, 0, N_GROUPS)
        gsel = jnp.logical_or(gsel, hit)
        gs = jnp.where(hit, neg, gs)
    emask = jnp.broadcast_to(gsel.reshape(N_GROUPS, 1, tm), (N_GROUPS, per, tm)).reshape(N_EXPERTS, tm)
    cand = jnp.where(emask, biased, neg)
    esel = jnp.zeros((N_EXPERTS, tm), jnp.bool_)
    for _ in range(TOP_K):
        _, hit = _first_max(cand, 0, N_EXPERTS)
        esel = jnp.logical_or(esel, hit)
        cand = jnp.where(hit, neg, cand)
    wsel = jnp.where(esel, scores, 0.0)
    return wsel / jnp.sum(wsel, axis=0, keepdims=True) * ROUTED_SCALE


def _post_kernel(o_ref, g_ref, bo_ref, ga_ref, bm_ref, x_ref,
                 gng_ref, gnb_ref, wo_ref, n2_ref, wsg_ref, wsu_ref, wsd_ref, wrt_ref, eb_ref,
                 base_ref, hn_ref, gate_ref):
    seg, exp = _seg_mats()
    o = o_ref[...]
    inv_n = 1.0 / HEAD_DIM
    mean = _mm_exact_rhs(_mm_exact_rhs(o, seg, passes=2) * inv_n, exp, passes=2)
    d = o - mean
    var = _mm_exact_rhs(d * d, seg, passes=1) * inv_n
    rstd = _mm_exact_rhs(lax.rsqrt(var + GN_EPS), exp, passes=1)
    ogn = d * rstd * gng_ref[...] + gnb_ref[...]
    out_a = (ogn + bo_ref[...]) * g_ref[...]
    merged = ga_ref[...] * out_a + bm_ref[...]
    h = x_ref[...] + _dg(merged.astype(BF16), wo_ref[...], NN)
    hn = _rmsnorm(h, n2_ref[...])
    hb = hn.astype(BF16)
    sg = _dg(hb, wsg_ref[...], NN)
    su = _dg(hb, wsu_ref[...], NN)
    shared = _dg((sg * _sigmoid(sg) * su).astype(BF16), wsd_ref[...], NN)
    base_ref[...] = h + shared
    hn_ref[...] = hb
    logits = _mm3(wrt_ref[...], hn, NT)
    scores = _sigmoid(logits)
    gate_t = _route(scores, scores + eb_ref[...])
    tm = gate_t.shape[1]
    gate_pad = jnp.concatenate([gate_t, jnp.zeros((LANES - N_EXPERTS, tm), F32)], axis=0)
    gate_ref[...] = gate_pad.T


def _post(o, g, bonus, ga, bm, x, p, tm):
    N, D = x.shape
    tm = min(tm, N)
    params = (p['gn_g'], p['gn_b'], p['w_out'], p['norm2_g'], p['ws_gate'], p['ws_up'], p['ws_down'],
              p['w_router_t'], p['e_bias'])
    row = pl.BlockSpec((tm, D), lambda i: (i, 0))
    return pl.pallas_call(
        _post_kernel,
        grid=(N // tm,),
        in_specs=[row] * 6 + [_full(a.shape) for a in params],
        out_specs=[row, row, pl.BlockSpec((tm, LANES), lambda i: (i, 0))],
        out_shape=[jax.ShapeDtypeStruct((N, D), F32), jax.ShapeDtypeStruct((N, D), BF16),
                   jax.ShapeDtypeStruct((N, LANES), F32)],
        compiler_params=_cparams(("parallel",)),
        name="post",
    )(o, g, bonus, ga, bm, x, *params)


def _moe_kernel(x_ref, gate_ref, base_ref, wg_ref, wu_ref, wd_ref, nf_ref, y_ref):
    j = pl.program_id(1)
    nj = pl.num_programs(1)
    eps = wg_ref.shape[0]
    gate = gate_ref[...]
    lane = lax.broadcasted_iota(jnp.int32, gate.shape, 1)
    x = x_ref[...]
    cols = [jnp.sum(jnp.where(lane == j * eps + q, gate, 0.0), axis=1, keepdims=True) for q in range(eps)]
    hg = [_dg(x, wg_ref[q], NN) for q in range(eps)]
    hu = [_dg(x, wu_ref[q], NN) for q in range(eps)]
    hh = [(hg[q] * _sigmoid(hg[q]) * hu[q] * cols[q]).astype(BF16) for q in range(eps)]
    down = _dg(jnp.concatenate(hh, axis=1), wd_ref[...].reshape(eps * D_EXPERT, D_MODEL), NN)

    @pl.when(j == 0)
    def _():
        y_ref[...] = base_ref[...] + down

    @pl.when(j > 0)
    def _():
        y_ref[...] += down

    @pl.when(j == nj - 1)
    def _():
        y_ref[...] = _rmsnorm(y_ref[...], nf_ref[...])


def _moe(hn, gate, base, p, tm, eps):
    N, D = base.shape
    tm = min(tm, N)
    row = lambda w: pl.BlockSpec((tm, w), lambda i, e: (i, 0))
    return pl.pallas_call(
        _moe_kernel,
        grid=(N // tm, N_EXPERTS // eps),
        in_specs=[row(D), row(LANES), row(D),
                  pl.BlockSpec((eps, D, D_EXPERT), lambda i, e: (e, 0, 0)),
                  pl.BlockSpec((eps, D, D_EXPERT), lambda i, e: (e, 0, 0)),
                  pl.BlockSpec((eps, D_EXPERT, D), lambda i, e: (e, 0, 0)),
                  pl.BlockSpec((1, D), lambda i, e: (0, 0))],
        out_specs=row(D),
        out_shape=jax.ShapeDtypeStruct((N, D), F32),
        compiler_params=_cparams(("parallel", "arbitrary")),
        name="moe",
    )(hn, gate, base, p['w_gate'], p['w_up'], p['w_down'], p['normf_g'])


def kernel(x_prompt, x_sample, state_wkv, state_shift, state_conv, norm1_g, w_in, mu_shift, w0, w_decay_up, a0, a_up, g_up, k_k, k_a, r_k, gn_g, gn_b, conv_w, conv_b, cln_g, cln_b, w_out, norm2_g, w_router, e_bias, w_gate, w_up, w_down, ws_gate, ws_up, ws_down, normf_g):
    depth = w_in.shape[0]
    assert depth == 1
    B, T, D = x_prompt.shape
    NS = x_sample.shape[0]
    assert x_sample.shape[1] == 1 and D == D_MODEL and T % CHUNK == 0
    row = lambda a: a[0].reshape(1, -1)
    p = {
        'norm1_g': row(norm1_g), 'mu_shift': row(mu_shift), 'w0': row(w0), 'a0': row(a0),
        'k_k': row(k_k), 'k_a': row(k_a), 'r_k': row(r_k), 'gn_g': row(gn_g), 'gn_b': row(gn_b),
        'conv_b': row(conv_b), 'cln_g': row(cln_g), 'cln_b': row(cln_b), 'norm2_g': row(norm2_g),
        'normf_g': normf_g.reshape(1, -1),
        'w_rwkv': w_in[0, :, :C_RWKV].astype(BF16), 'w_rest': w_in[0, :, C_RWKV:].astype(BF16),
        'w_decay_up': w_decay_up[0], 'a_up': a_up[0], 'g_up': g_up[0], 'conv_w': conv_w[0],
        'w_out': w_out[0].astype(BF16),
        'ws_gate': ws_gate[0].astype(BF16), 'ws_up': ws_up[0].astype(BF16), 'ws_down': ws_down[0].astype(BF16),
        'w_router_t': w_router[0].T, 'e_bias': e_bias[0].reshape(-1, 1),
        'w_gate': w_gate[0].astype(BF16), 'w_up': w_up[0].astype(BF16), 'w_down': w_down[0].astype(BF16),
    }

    zp0 = jnp.zeros((B, 1, C_RWKV), F32)
    r, lw, k, v, kk, b, g, bonus, shift_p, ga, bm, conv_p = _mix_seq(
        x_prompt, zp0, jnp.zeros((B, CONV_WIDTH - 1, D), F32), p, tm=256)
    o, wkv_p = _wkv(r, lw, k, v, kk, b, rows=4 * CHUNK)
    flat = lambda a: a.reshape(B * T, D)
    base, hn, gate = _post(flat(o), flat(g), flat(bonus), flat(ga), flat(bm), flat(x_prompt), p, tm=512)
    y_prompt = _moe(hn, gate, base, p, tm=1024, eps=4).reshape(B, T, D)

    xs = x_sample.reshape(NS, D)
    r, lw, k, v, kk, b, g, bonus, shift_s = _rwkv_prep_batch(xs, state_shift[0], p)
    ga, bm, conv_s_t = _conv_branch_batch(xs, jnp.swapaxes(state_conv[0], 0, 1), p)
    o_t, wkv_s_t = _wkv_step(jnp.transpose(state_wkv[0], (1, 2, 3, 0)),
                             jnp.transpose(jnp.stack([r, lw, k, v, kk, b]), (0, 2, 1)))
    o = o_t.T
    wkv_s = jnp.transpose(wkv_s_t, (3, 0, 1, 2))
    base, hn, gate = _post(o, g, bonus, ga, bm, xs, p, tm=128)
    y_sample = _moe(hn, gate, base, p, tm=128, eps=4).reshape(NS, 1, D)

    return (y_prompt, y_sample, wkv_p[None], shift_p.reshape(1, B, D), conv_p[None],
            wkv_s[None], shift_s[None], jnp.swapaxes(conv_s_t, 0, 1)[None])
```

```python
import functools

import jax
import jax.numpy as jnp
from jax import lax
from jax.experimental import pallas as pl
from jax.experimental.pallas import tpu as pltpu

F32 = jnp.float32
BF16 = jnp.bfloat16

D_MODEL = 1024
HEAD_DIM = 64
N_HEADS = D_MODEL // HEAD_DIM
D_DECAY_LORA = 64
D_AAA_LORA = 64
D_GATE_LORA = 128
GN_EPS = 64e-5
CONV_WIDTH = 31
LN_EPS = 1e-5
N_EXPERTS = 64
N_GROUPS = 8
TOPK_GROUPS = 4
TOP_K = 8
D_EXPERT = 256
ROUTED_SCALE = 2.5
RMS_EPS = 1e-6

O_K = D_MODEL
O_V = 2 * D_MODEL
O_W = 3 * D_MODEL
O_A = O_W + D_DECAY_LORA
O_G = O_A + D_AAA_LORA
C_RWKV = O_G + D_GATE_LORA
C_REST = 4 * D_MODEL

LANES = 128
SUBLANES = 8
CHUNK = 64
PAIR = 2 * HEAD_DIM
VMEM_LIMIT = 56 * 1024 * 1024

NN = ((1,), (0,))
NT = ((1,), (1,))
TN = ((0,), (0,))


def _dg(a, b, dims):
    return lax.dot_general(a, b, (dims, ((), ())), preferred_element_type=F32)


def _split2(x):
    hi = x.astype(BF16)
    lo = (x - hi.astype(F32)).astype(BF16)
    return hi, lo


def _split3(x):
    hi = x.astype(BF16)
    r1 = x - hi.astype(F32)
    mid = r1.astype(BF16)
    lo = (r1 - mid.astype(F32)).astype(BF16)
    return hi, mid, lo


def _mm1(a, b, dims=NN):
    return _dg(a.astype(BF16), b.astype(BF16), dims)


def _mm3(a, b, dims=NN):
    ah, al = _split2(a)
    bh, bl = _split2(b)
    m = a.shape[0]
    if dims[0] == (1,) and m % SUBLANES == 0:
        top = _dg(jnp.concatenate([ah, al], axis=0), bh, dims)
        return top[0:m] + (_dg(ah, bl, dims) + top[m:2 * m])
    return _dg(ah, bh, dims) + (_dg(ah, bl, dims) + _dg(al, bh, dims))


def _mm_exact_rhs(a, b_bf16, dims=NN, passes=3):
    if passes == 1:
        return _dg(a.astype(BF16), b_bf16, dims)
    if passes == 2:
        h, l = _split2(a)
        return _dg(h, b_bf16, dims) + _dg(l, b_bf16, dims)
    h, m, l = _split3(a)
    return _dg(h, b_bf16, dims) + (_dg(m, b_bf16, dims) + _dg(l, b_bf16, dims))


def _rmsnorm(x, g):
    return x * lax.rsqrt(jnp.mean(x * x, axis=-1, keepdims=True) + RMS_EPS) * g


def _sigmoid(x):
    return 1.0 / (1.0 + jnp.exp(-x))


def _seg_mats():
    row = lax.broadcasted_iota(jnp.int32, (D_MODEL, LANES), 0) // HEAD_DIM
    col = lax.broadcasted_iota(jnp.int32, (D_MODEL, LANES), 1)
    seg = (row == col).astype(BF16)
    rowt = lax.broadcasted_iota(jnp.int32, (LANES, D_MODEL), 0)
    colt = lax.broadcasted_iota(jnp.int32, (LANES, D_MODEL), 1) // HEAD_DIM
    exp = (rowt == colt).astype(BF16)
    return seg, exp


def _cparams(sem):
    return pltpu.CompilerParams(dimension_semantics=sem, vmem_limit_bytes=VMEM_LIMIT)


def _full(shape):
    n = len(shape)
    return pl.BlockSpec(shape, lambda *_: (0,) * n)


def _prep_math(zr, zp, mu, w0, wdu, a0, aup, gup, kk_w, ka_w, rk_w, seg, exp):
    zm = zr + (zp - zr) * mu
    r = zm[:, 0:O_K]
    k = zm[:, O_K:O_V]
    v = zm[:, O_V:O_W]
    xw = jnp.tanh(zm[:, O_W:O_A])
    xa = zm[:, O_A:O_G]
    xg = _sigmoid(zm[:, O_G:C_RWKV])
    y = -(w0 + _mm1(xw, wdu))
    softplus = jnp.maximum(y, 0.0) + jnp.log(1.0 + jnp.exp(-jnp.abs(y)))
    lw = -jnp.exp(-softplus - 0.5)
    a = _sigmoid(a0 + _mm1(xa, aup))
    g = _mm1(xg, gup)
    kkr = k * kk_w
    ss = _mm_exact_rhs(kkr * kkr, seg, passes=2)
    inv = 1.0 / jnp.maximum(jnp.sqrt(ss), 1e-12)
    kk = kkr * _mm_exact_rhs(inv, exp, passes=2)
    kf = k * (1.0 + (a - 1.0) * ka_w)
    b = kk * a
    rk = _mm_exact_rhs(r * kf * rk_w, seg, passes=1)
    bonus = _mm_exact_rhs(rk, exp, passes=1) * v
    return r, lw, kf, v, kk, b, g, bonus


def _prep_seq_body(zr, vec_refs, carry_ref):
    tm = zr.shape[0]
    rows = lax.broadcasted_iota(jnp.int32, zr.shape, 0)
    zp = jnp.where(rows == 0, carry_ref[...], pltpu.roll(zr, 1, 0))
    carry_ref[...] = zr[tm - 1:tm, :]
    seg, exp = _seg_mats()
    return _prep_math(zr, zp, *[ref[...] for ref in vec_refs], seg, exp)


def _prep_batch_kernel(x_ref, xp_ref, g1_ref, w_ref, mu_ref, w0_ref, wdu_ref, a0_ref, aup_ref, gup_ref,
                       kkw_ref, kaw_ref, rkw_ref,
                       r_ref, lw_ref, k_ref, v_ref, kk_ref, b_ref, g_ref, bo_ref, xn_ref):
    xn = _rmsnorm(x_ref[...], g1_ref[...])
    w = w_ref[...]
    zr = _dg(xn.astype(BF16), w, NN)
    zp = _dg(xp_ref[...].astype(BF16), w, NN)
    seg, exp = _seg_mats()
    outs = _prep_math(zr, zp, mu_ref[...], w0_ref[...], wdu_ref[...], a0_ref[...], aup_ref[...], gup_ref[...],
                      kkw_ref[...], kaw_ref[...], rkw_ref[...], seg, exp)
    for o_ref, val in zip((r_ref, lw_ref, k_ref, v_ref, kk_ref, b_ref, g_ref, bo_ref), outs):
        o_ref[...] = val
    xn_ref[...] = xn


def _prep_params(p):
    return (p['norm1_g'], p['w_rwkv'], p['mu_shift'], p['w0'], p['w_decay_up'], p['a0'], p['a_up'], p['g_up'],
            p['k_k'], p['k_a'], p['r_k'])


def _rwkv_prep_batch(x, xprev, p):
    N, D = x.shape
    params = _prep_params(p)
    out_shape = [jax.ShapeDtypeStruct((N, D), F32)] * 9
    return pl.pallas_call(
        _prep_batch_kernel,
        grid=(1,),
        in_specs=[_full(x.shape), _full(xprev.shape)] + [_full(a.shape) for a in params],
        out_specs=[_full((N, D))] * 9,
        out_shape=out_shape,
        compiler_params=_cparams(("arbitrary",)),
        name="rwkv_prep_batch",
    )(x, xprev, *params)


HIST = 32


def _conv_tail(c, cb, lg, lb, mix_a, mix_b):
    c = c + cb
    mean = jnp.mean(c, axis=-1, keepdims=True)
    d = c - mean
    var = jnp.mean(d * d, axis=-1, keepdims=True)
    y = d * lax.rsqrt(var + LN_EPS) * lg + lb
    out_b = y * _sigmoid(y)
    return _sigmoid(mix_a), _sigmoid(mix_b) * out_b


def _glu_mix(xn, w_ref):
    xb = xn.astype(BF16)
    glu_a = _dg(xb, w_ref[:, 0:D_MODEL], NN)
    glu_b = _dg(xb, w_ref[:, D_MODEL:2 * D_MODEL], NN)
    mix_a = _dg(xb, w_ref[:, 2 * D_MODEL:3 * D_MODEL], NN)
    mix_b = _dg(xb, w_ref[:, 3 * D_MODEL:4 * D_MODEL], NN)
    return glu_a * _sigmoid(glu_b), mix_a, mix_b


def _conv_taps(u, cw_ref, ubuf_ref):
    tm = u.shape[0]
    npast = CONV_WIDTH - 1
    ubuf_ref[pl.ds(HIST, tm), :] = u
    c = cw_ref[pl.ds(npast, 1), :] * ubuf_ref[pl.ds(HIST, tm), :]
    for s in range(SUBLANES):
        offs = [o for o in range(HIST - npast, HIST) if o % SUBLANES == s]
        grp = None
        for o in offs:
            term = cw_ref[pl.ds(o - (HIST - npast), 1), :] * ubuf_ref[pl.ds(o - s, tm + SUBLANES), :]
            grp = term if grp is None else grp + term
        c = c + grp[s:s + tm, :]
    return c


def _mix_seq_kernel(x_ref, zp0_ref, sc_ref, g1_ref, wr_ref, mu_ref, w0_ref, wdu_ref, a0_ref, aup_ref, gup_ref,
                    kkw_ref, kaw_ref, rkw_ref, wc_ref, cw_ref, cb_ref, lg_ref, lb_ref,
                    r_ref, lw_ref, k_ref, v_ref, kk_ref, b_ref, g_ref, bo_ref, xl_ref, ga_ref, bm_ref, so_ref,
                    carry_ref, ubuf_ref):
    t = pl.program_id(1)
    nt = pl.num_programs(1)
    tm = x_ref.shape[1]
    npast = CONV_WIDTH - 1

    @pl.when(t == 0)
    def _():
        carry_ref[...] = zp0_ref[0]
        ubuf_ref[pl.ds(0, HIST - npast), :] = jnp.zeros((HIST - npast, D_MODEL), F32)
        ubuf_ref[pl.ds(HIST - npast, npast), :] = sc_ref[0]

    @pl.when(t > 0)
    def _():
        ubuf_ref[pl.ds(0, HIST), :] = ubuf_ref[pl.ds(tm, HIST), :]

    xn = _rmsnorm(x_ref[0], g1_ref[...])
    xb = xn.astype(BF16)
    glu_a = _dg(xb, wc_ref[:, 0:D_MODEL], NN)
    glu_b = _dg(xb, wc_ref[:, D_MODEL:2 * D_MODEL], NN)
    c = _conv_taps(glu_a * _sigmoid(glu_b), cw_ref, ubuf_ref)
    zr = _dg(xb, wr_ref[...], NN)
    mix_a = _dg(xb, wc_ref[:, 2 * D_MODEL:3 * D_MODEL], NN)
    mix_b = _dg(xb, wc_ref[:, 3 * D_MODEL:4 * D_MODEL], NN)
    ga, bm = _conv_tail(c, cb_ref[...], lg_ref[...], lb_ref[...], mix_a, mix_b)
    outs = _prep_seq_body(zr, (mu_ref, w0_ref, wdu_ref, a0_ref, aup_ref, gup_ref, kkw_ref, kaw_ref, rkw_ref), carry_ref)
    for o_ref, val in zip((r_ref, lw_ref, k_ref, v_ref, kk_ref, b_ref, g_ref, bo_ref), outs):
        o_ref[0] = val
    xl_ref[0] = xn[tm - 1:tm, :]
    ga_ref[0] = ga
    bm_ref[0] = bm

    @pl.when(t == nt - 1)
    def _():
        so_ref[0] = ubuf_ref[pl.ds(tm + HIST - npast, npast), :]


def _conv_batch_kernel(x_ref, sc_ref, g1_ref, w_ref, cw_ref, cb_ref, lg_ref, lb_ref,
                       ga_ref, bm_ref, so_ref):
    npast = CONV_WIDTH - 1
    xn = _rmsnorm(x_ref[...], g1_ref[...])
    u, mix_a, mix_b = _glu_mix(xn, w_ref)
    c = cw_ref[pl.ds(npast, 1), :] * u
    for j in range(npast):
        c = c + cw_ref[pl.ds(j, 1), :] * sc_ref[j]
    ga, bm = _conv_tail(c, cb_ref[...], lg_ref[...], lb_ref[...], mix_a, mix_b)
    ga_ref[...] = ga
    bm_ref[...] = bm
    for j in range(npast - 1):
        so_ref[j] = sc_ref[j + 1]
    so_ref[npast - 1] = u


def _conv_params(p):
    return (p['norm1_g'], p['w_rest'], p['conv_w'], p['conv_b'], p['cln_g'], p['cln_b'])


def _resident(shape):
    n = len(shape)
    return pl.BlockSpec(shape, lambda *_: (0,) * n, pipeline_mode=pl.Buffered(1))


def _mix_seq(x, zp0, s_conv, p, tm):
    B, T, D = x.shape
    tm = min(tm, T)
    params = _prep_params(p) + _conv_params(p)[1:]
    npast = CONV_WIDTH - 1
    seq = pl.BlockSpec((1, tm, D), lambda b, t: (b, t, 0))
    per_b = lambda n: pl.BlockSpec((1, n, D), lambda b, t: (b, 0, 0))
    return pl.pallas_call(
        _mix_seq_kernel,
        grid=(B, T // tm),
        in_specs=[seq, pl.BlockSpec((1, 1, C_RWKV), lambda b, t: (b, 0, 0)), per_b(npast)]
                 + [_resident(a.shape) for a in params],
        out_specs=[seq] * 8 + [per_b(1), seq, seq, per_b(npast)],
        out_shape=[jax.ShapeDtypeStruct((B, T, D), F32)] * 8 + [jax.ShapeDtypeStruct((B, 1, D), F32)]
                  + [jax.ShapeDtypeStruct((B, T, D), F32)] * 2 + [jax.ShapeDtypeStruct((B, npast, D), F32)],
        scratch_shapes=[pltpu.VMEM((1, C_RWKV), F32), pltpu.VMEM((tm + HIST, D), F32)],
        compiler_params=_cparams(("parallel", "arbitrary")),
        name="mix_seq",
    )(x, zp0, s_conv, *params)


def _conv_branch_batch(x, s_conv_t, p):
    N, D = x.shape
    params = _conv_params(p)
    return pl.pallas_call(
        _conv_batch_kernel,
        grid=(1,),
        in_specs=[_full(x.shape), _full(s_conv_t.shape)] + [_full(a.shape) for a in params],
        out_specs=[_full((N, D)), _full((N, D)), _full(s_conv_t.shape)],
        out_shape=[jax.ShapeDtypeStruct((N, D), F32)] * 2 + [jax.ShapeDtypeStruct(s_conv_t.shape, F32)],
        compiler_params=_cparams(("arbitrary",)),
        name="conv_branch_batch",
    )(x, s_conv_t, *params)


def _pair_masks(shape):
    lane = lax.broadcasted_iota(jnp.int32, shape, 1)
    return lane < HEAD_DIM


def _bd(y, m0):
    zero = jnp.zeros_like(y)
    return jnp.concatenate([jnp.where(m0, y, zero), jnp.where(m0, zero, y)], axis=0)


def _bdmm(x, y, m0, mm):
    return mm(x, _bd(y, m0), NN)


def _bdmm_nt(x, y, m0, mm):
    return mm(x, _bd(y, m0), NT)


def _bdmm_tn(x, y, m0, mm):
    a = mm(x, y, TN)
    return jnp.where(m0, a[0:HEAD_DIM, :], a[HEAD_DIM:PAIR, :])


def _map(f, *lists):
    return [f(*xs) for xs in zip(*lists)]


def _chunk_pairs(r, cum, lw, k, v, kk, b, hooks=()):
    L = CHUNK
    hooks = list(hooks)

    def run_hook():
        if hooks:
            hooks.pop(0)()

    m0 = _pair_masks((L, PAIR))
    trow = lax.broadcasted_iota(jnp.int32, (L, PAIR), 0)
    icol = lax.broadcasted_iota(jnp.int32, (L, PAIR), 1) % HEAD_DIM
    strict = icol < trow
    incl = icol <= trow
    eye = (icol == trow).astype(F32)
    zero = jnp.zeros((L, PAIR), F32)

    cl = [c[L - 1:L, :] for c in cum]
    alpha = _map(lambda x, c, w: x * jnp.exp(c - w), kk, cum, lw)
    rho = _map(lambda x, c: x * jnp.exp(c), r, cum)
    einv = [jnp.exp(-c) for c in cum]
    kappa = _map(lambda x, e: x * e, k, einv)
    beta = _map(lambda x, e: x * e, b, einv)
    etail = _map(lambda c1, c: jnp.exp(c1 - c), cl, cum)
    kappa2 = _map(lambda x, e: x * e, k, etail)
    beta2 = _map(lambda x, e: x * e, b, etail)
    dl = [jnp.exp(c1) for c1 in cl]

    mm = _mm1

    def rows2(x0, x1, y, dims):
        out = mm(jnp.concatenate([x0, x1], axis=0), _bd(y, m0), dims)
        return out[0:L, :], out[L:2 * L, :]

    def cols2(x, y0, y1):
        out = mm(x, jnp.concatenate([_bd(y0, m0), _bd(y1, m0)], axis=1), NN)
        return out[:, 0:PAIR], out[:, PAIR:2 * PAIR]

    def scores(a, q, y0, y1):
        out = mm(jnp.concatenate([a, q], axis=0), jnp.concatenate([_bd(y0, m0), _bd(y1, m0)], axis=0), NT)
        return out[0:L, 0:PAIR], out[L:2 * L, 0:PAIR], out[0:L, PAIR:2 * PAIR], out[L:2 * L, PAIR:2 * PAIR]

    sc = _map(scores, alpha, rho, kappa, beta)
    m_k = [jnp.where(strict, x[0], zero) for x in sc]
    n_k = [jnp.where(incl, x[1], zero) for x in sc]
    m_b = [jnp.where(strict, x[2], zero) for x in sc]
    n_b = [jnp.where(incl, x[3], zero) for x in sc]
    run_hook()

    nn = [-m for m in m_b]
    tinv = [eye + q for q in nn]
    pw = _map(lambda q: _bdmm(q, q, m0, mm), nn)
    mvn = _map(lambda a, q, y: rows2(a, q, y, NN), m_k, n_k, v)
    for it in range(4):
        res = _map(lambda q, t: rows2(q, t, q, NN), pw, tinv)
        tinv = _map(lambda t, x: t + x[1], tinv, res)
        pw = [x[0] for x in res]
        if it % 2 == 1:
            run_hook()
    tinv = _map(lambda t, q: t + _bdmm(t, q, m0, mm), tinv, pw)

    mv = [x[0] for x in mvn]
    nkv = [x[1] for x in mvn]
    aw = _map(cols2, tinv, alpha, mv)
    alpha2 = [x[0] for x in aw]
    w = [x[1] for x in aw]
    run_hook()
    nb = _map(cols2, n_b, alpha2, w)
    rho2 = _map(lambda x, y: x - y[0], rho, nb)
    o2 = _map(lambda x, y: x - y[1], nkv, nb)

    def tn2(x0, x1, y):
        a = mm(jnp.concatenate([x0, x1], axis=1), y, TN)
        pick = lambda z: jnp.where(m0, z[0:HEAD_DIM, :], z[HEAD_DIM:PAIR, :])
        return pick(a[0:PAIR, :]), pick(a[PAIR:2 * PAIR, :])

    ab = _map(tn2, alpha2, w, beta2)
    g = _map(lambda d, x: eye * d - x[0], dl, ab)
    h = _map(lambda x, k2, y: _bdmm_tn(x, k2, m0, mm) - y[1], v, kappa2, ab)
    while hooks:
        run_hook()
    return rho2, o2, g, h


def _wkv_kernel(r_ref, lw_ref, k_ref, v_ref, kk_ref, b_ref, o_ref, s_out_ref, rho_s, o2_s, g_s, h_s, s_ref):
    j = pl.program_id(1)
    n = pl.num_programs(1)
    L = CHUNK
    rows, width = r_ref.shape[1], r_ref.shape[2]
    nsub = rows // L
    m0 = _pair_masks((L, PAIR))
    sls = [slice(p * PAIR, (p + 1) * PAIR) for p in range(width // PAIR)]

    @pl.when(j == 0)
    def _():
        for ref in (rho_s, o2_s, g_s, h_s, s_ref):
            ref[...] = jnp.zeros_like(ref)

    state = [s_ref[:, sl] for sl in sls]

    def recur(c, row0):
        rs = pl.ds(c * L, L)
        o = [_bdmm_nt(rho_s[rs, sl], x, m0, _mm1) + o2_s[rs, sl] for sl, x in zip(sls, state)]
        state[:] = [_bdmm(x, g_s[rs, sl], m0, _mm3) + h_s[rs, sl] for sl, x in zip(sls, state)]
        dst = pl.ds(pl.multiple_of(row0 + c * L, L), L)
        for sl, ov in zip(sls, o):
            o_ref[0, dst, sl] = ov

    prev_row0 = jnp.maximum(j - 1, 0) * rows
    hooks = [functools.partial(recur, c, prev_row0) for c in range(nsub)]

    items = [(pl.ds(c * L, L), sl) for c in range(nsub) for sl in sls]
    lw = [lw_ref[0, rs, sl] for rs, sl in items]
    trow = lax.broadcasted_iota(jnp.int32, (L, PAIR), 0)
    cum = lw
    shift = 1
    while shift < L:
        cum = [x + jnp.where(trow >= shift, pltpu.roll(x, shift, 0), 0.0) for x in cum]
        shift *= 2
    pick = lambda ref: [ref[0, rs, sl] for rs, sl in items]
    outs = _chunk_pairs(pick(r_ref), cum, lw, pick(k_ref), pick(v_ref), pick(kk_ref), pick(b_ref), hooks)
    for ref, vals in zip((rho_s, o2_s, g_s, h_s), outs):
        for (rs, sl), val in zip(items, vals):
            ref[rs, sl] = val
    for sl, sv in zip(sls, state):
        s_ref[:, sl] = sv

    @pl.when(j == n - 1)
    def _():
        state[:] = [s_ref[:, sl] for sl in sls]
        for c in range(nsub):
            recur(c, j * rows)
        for p, sv in enumerate(state):
            s_out_ref[0, 2 * p] = sv[:, :HEAD_DIM]
            s_out_ref[0, 2 * p + 1] = sv[:, HEAD_DIM:]


def _wkv(r, lw, k, v, kk, b, rows):
    B, T, D = r.shape
    blk = pl.BlockSpec((1, rows, D), lambda bi, c: (bi, c, 0))
    return pl.pallas_call(
        _wkv_kernel,
        grid=(B, T // rows),
        in_specs=[blk] * 6,
        out_specs=[pl.BlockSpec((1, T, D), lambda bi, c: (bi, 0, 0)),
                   pl.BlockSpec((1, N_HEADS, HEAD_DIM, HEAD_DIM), lambda bi, c: (bi, 0, 0, 0))],
        out_shape=[jax.ShapeDtypeStruct((B, T, D), F32),
                   jax.ShapeDtypeStruct((B, N_HEADS, HEAD_DIM, HEAD_DIM), F32)],
        scratch_shapes=[pltpu.VMEM((rows, D), F32)] * 4 + [pltpu.VMEM((HEAD_DIM, D), F32)],
        compiler_params=_cparams(("parallel", "arbitrary")),
        name="wkv",
    )(r, lw, k, v, kk, b)


def _wkv_step_kernel(s_ref, vec_ref, o_ref, so_ref):
    r, lw, k, v, kk, b = [vec_ref[i] for i in range(6)]
    d = jnp.exp(lw)
    for vi in range(HEAD_DIM):
        s = s_ref[0, vi]
        sa = -jnp.sum(s * kk, axis=0, keepdims=True)
        s_new = s * d + sa * b + v[vi:vi + 1, :] * k
        so_ref[0, vi] = s_new
        o_ref[pl.ds(vi, 1), :] = jnp.sum(s_new * r, axis=0, keepdims=True)


def _wkv_step(s_t, vecs_t):
    n = s_t.shape[-1]
    st = pl.BlockSpec((1, HEAD_DIM, HEAD_DIM, n), lambda h: (h, 0, 0, 0))
    return pl.pallas_call(
        _wkv_step_kernel,
        grid=(N_HEADS,),
        in_specs=[st, pl.BlockSpec((6, HEAD_DIM, n), lambda h: (0, h, 0))],
        out_specs=[pl.BlockSpec((HEAD_DIM, n), lambda h: (h, 0)), st],
        out_shape=[jax.ShapeDtypeStruct((D_MODEL, n), F32), jax.ShapeDtypeStruct(s_t.shape, F32)],
        compiler_params=_cparams(("parallel",)),
        name="wkv_step",
    )(s_t, vecs_t)


def _first_max(x, axis, n):
    m = jnp.max(x, axis=axis, keepdims=True)
    idx = lax.broadcasted_iota(jnp.int32, x.shape, axis)
    first = jnp.min(jnp.where(x == m, idx, n), axis=axis, keepdims=True)
    return m, idx == first


def _route(scores, biased):
    tm = scores.shape[1]
    per = N_EXPERTS // N_GROUPS
    neg = jnp.full((), -jnp.inf, F32)
    b3 = biased.reshape(N_GROUPS, per, tm)
    m1, hit = _first_max(b3, 1, per)
    m2 = jnp.max(jnp.where(hit, neg, b3), axis=1, keepdims=True)
    gs = (m1 + m2).reshape(N_GROUPS, tm)
    gsel = jnp.zeros((N_GROUPS, tm), jnp.bool_)
    for _ in range(TOPK_GROUPS):
        _, hit = _first_max(gs, 0, N_GROUPS)
        gsel = jnp.logical_or(gsel, hit)
        gs = jnp.where(hit, neg, gs)
    emask = jnp.broadcast_to(gsel.reshape(N_GROUPS, 1, tm), (N_GROUPS, per, tm)).reshape(N_EXPERTS, tm)
    cand = jnp.where(emask, biased, neg)
    esel = jnp.zeros((N_EXPERTS, tm), jnp.bool_)
    for _ in range(TOP_K):
        _, hit = _first_max(cand, 0, N_EXPERTS)
        esel = jnp.logical_or(esel, hit)
        cand = jnp.where(hit, neg, cand)
    wsel = jnp.where(esel, scores, 0.0)
    return wsel / jnp.sum(wsel, axis=0, keepdims=True) * ROUTED_SCALE


def _post_kernel(o_ref, g_ref, bo_ref, ga_ref, bm_ref, x_ref,
                 gng_ref, gnb_ref, wo_ref, n2_ref, wsg_ref, wsu_ref, wsd_ref, wrt_ref, eb_ref,
                 base_ref, hn_ref, gate_ref):
    seg, exp = _seg_mats()
    o = o_ref[...]
    inv_n = 1.0 / HEAD_DIM
    mean = _mm_exact_rhs(_mm_exact_rhs(o, seg, passes=1) * inv_n, exp, passes=2)
    d = o - mean
    var = _mm_exact_rhs(d * d, seg, passes=1) * inv_n
    rstd = _mm_exact_rhs(lax.rsqrt(var + GN_EPS), exp, passes=1)
    ogn = d * rstd * gng_ref[...] + gnb_ref[...]
    out_a = (ogn + bo_ref[...]) * g_ref[...]
    merged = ga_ref[...] * out_a + bm_ref[...]
    h = x_ref[...] + _dg(merged.astype(BF16), wo_ref[...], NN)
    hn = _rmsnorm(h, n2_ref[...])
    hb = hn.astype(BF16)
    sg = _dg(hb, wsg_ref[...], NN)
    su = _dg(hb, wsu_ref[...], NN)
    shared = _dg((sg * _sigmoid(sg) * su).astype(BF16), wsd_ref[...], NN)
    base_ref[...] = h + shared
    hn_ref[...] = hb
    logits = _mm3(wrt_ref[...], hn, NT)
    scores = _sigmoid(logits)
    gate_t = _route(scores, scores + eb_ref[...])
    tm = gate_t.shape[1]
    gate_pad = jnp.concatenate([gate_t, jnp.zeros((LANES - N_EXPERTS, tm), F32)], axis=0)
    gate_ref[...] = gate_pad.T


def _post(o, g, bonus, ga, bm, x, p, tm):
    N, D = x.shape
    tm = min(tm, N)
    params = (p['gn_g'], p['gn_b'], p['w_out'], p['norm2_g'], p['ws_gate'], p['ws_up'], p['ws_down'],
              p['w_router_t'], p['e_bias'])
    row = pl.BlockSpec((tm, D), lambda i: (i, 0))
    return pl.pallas_call(
        _post_kernel,
        grid=(N // tm,),
        in_specs=[row] * 6 + [_full(a.shape) for a in params],
        out_specs=[row, row, pl.BlockSpec((tm, LANES), lambda i: (i, 0))],
        out_shape=[jax.ShapeDtypeStruct((N, D), F32), jax.ShapeDtypeStruct((N, D), BF16),
                   jax.ShapeDtypeStruct((N, LANES), F32)],
        compiler_params=_cparams(("parallel",)),
        name="post",
    )(o, g, bonus, ga, bm, x, *params)


def _moe_kernel(x_ref, gate_ref, base_ref, wg_ref, wu_ref, wd_ref, nf_ref, y_ref):
    j = pl.program_id(1)
    nj = pl.num_programs(1)
    eps = wg_ref.shape[0]
    gate = gate_ref[...]
    lane = lax.broadcasted_iota(jnp.int32, gate.shape, 1)
    x = x_ref[...]
    cols = [jnp.sum(jnp.where(lane == j * eps + q, gate, 0.0), axis=1, keepdims=True) for q in range(eps)]
    hg = [_dg(x, wg_ref[q], NN) for q in range(eps)]
    hu = [_dg(x, wu_ref[q], NN) for q in range(eps)]
    hh = [(hg[q] * _sigmoid(hg[q]) * hu[q] * cols[q]).astype(BF16) for q in range(eps)]
    down = _dg(jnp.concatenate(hh, axis=1), wd_ref[...].reshape(eps * D_EXPERT, D_MODEL), NN)

    @pl.when(j == 0)
    def _():
        y_ref[...] = base_ref[...] + down

    @pl.when(j > 0)
    def _():
        y_ref[...] += down

    @pl.when(j == nj - 1)
    def _():
        y_ref[...] = _rmsnorm(y_ref[...], nf_ref[...])


def _moe(hn, gate, base, p, tm, eps):
    N, D = base.shape
    tm = min(tm, N)
    row = lambda w: pl.BlockSpec((tm, w), lambda i, e: (i, 0))
    return pl.pallas_call(
        _moe_kernel,
        grid=(N // tm, N_EXPERTS // eps),
        in_specs=[row(D), row(LANES), row(D),
                  pl.BlockSpec((eps, D, D_EXPERT), lambda i, e: (e, 0, 0)),
                  pl.BlockSpec((eps, D, D_EXPERT), lambda i, e: (e, 0, 0)),
                  pl.BlockSpec((eps, D_EXPERT, D), lambda i, e: (e, 0, 0)),
                  pl.BlockSpec((1, D), lambda i, e: (0, 0))],
        out_specs=row(D),
        out_shape=jax.ShapeDtypeStruct((N, D), F32),
        compiler_params=_cparams(("parallel", "arbitrary")),
        name="moe",
    )(hn, gate, base, p['w_gate'], p['w_up'], p['w_down'], p['normf_g'])


def kernel(x_prompt, x_sample, state_wkv, state_shift, state_conv, norm1_g, w_in, mu_shift, w0, w_decay_up, a0, a_up, g_up, k_k, k_a, r_k, gn_g, gn_b, conv_w, conv_b, cln_g, cln_b, w_out, norm2_g, w_router, e_bias, w_gate, w_up, w_down, ws_gate, ws_up, ws_down, normf_g):
    depth = w_in.shape[0]
    assert depth == 1
    B, T, D = x_prompt.shape
    NS = x_sample.shape[0]
    assert x_sample.shape[1] == 1 and D == D_MODEL and T % CHUNK == 0
    row = lambda a: a[0].reshape(1, -1)
    p = {
        'norm1_g': row(norm1_g), 'mu_shift': row(mu_shift), 'w0': row(w0), 'a0': row(a0),
        'k_k': row(k_k), 'k_a': row(k_a), 'r_k': row(r_k), 'gn_g': row(gn_g), 'gn_b': row(gn_b),
        'conv_b': row(conv_b), 'cln_g': row(cln_g), 'cln_b': row(cln_b), 'norm2_g': row(norm2_g),
        'normf_g': normf_g.reshape(1, -1),
        'w_rwkv': w_in[0, :, :C_RWKV].astype(BF16), 'w_rest': w_in[0, :, C_RWKV:].astype(BF16),
        'w_decay_up': w_decay_up[0], 'a_up': a_up[0], 'g_up': g_up[0], 'conv_w': conv_w[0],
        'w_out': w_out[0].astype(BF16),
        'ws_gate': ws_gate[0].astype(BF16), 'ws_up': ws_up[0].astype(BF16), 'ws_down': ws_down[0].astype(BF16),
        'w_router_t': w_router[0].T, 'e_bias': e_bias[0].reshape(-1, 1),
        'w_gate': w_gate[0].astype(BF16), 'w_up': w_up[0].astype(BF16), 'w_down': w_down[0].astype(BF16),
    }

    zp0 = jnp.zeros((B, 1, C_RWKV), F32)
    r, lw, k, v, kk, b, g, bonus, shift_p, ga, bm, conv_p = _mix_seq(
        x_prompt, zp0, jnp.zeros((B, CONV_WIDTH - 1, D), F32), p, tm=256)
    o, wkv_p = _wkv(r, lw, k, v, kk, b, rows=4 * CHUNK)
    flat = lambda a: a.reshape(B * T, D)
    base, hn, gate = _post(flat(o), flat(g), flat(bonus), flat(ga), flat(bm), flat(x_prompt), p, tm=512)
    y_prompt = _moe(hn, gate, base, p, tm=1024, eps=4).reshape(B, T, D)

    xs = x_sample.reshape(NS, D)
    r, lw, k, v, kk, b, g, bonus, shift_s = _rwkv_prep_batch(xs, state_shift[0], p)
    ga, bm, conv_s_t = _conv_branch_batch(xs, jnp.swapaxes(state_conv[0], 0, 1), p)
    o_t, wkv_s_t = _wkv_step(jnp.transpose(state_wkv[0], (1, 2, 3, 0)),
                             jnp.transpose(jnp.stack([r, lw, k, v, kk, b]), (0, 2, 1)))
    o = o_t.T
    wkv_s = jnp.transpose(wkv_s_t, (3, 0, 1, 2))
    base, hn, gate = _post(o, g, bonus, ga, bm, xs, p, tm=128)
    y_sample = _moe(hn, gate, base, p, tm=128, eps=4).reshape(NS, 1, D)

    return (y_prompt, y_sample, wkv_p[None], shift_p.reshape(1, B, D), conv_p[None],
            wkv_s[None], shift_s[None], jnp.swapaxes(conv_s_t, 0, 1)[None])
```

```python
import functools

import jax
import jax.numpy as jnp
from jax import lax
from jax.experimental import pallas as pl
from jax.experimental.pallas import tpu as pltpu

F32 = jnp.float32
BF16 = jnp.bfloat16

D_MODEL = 1024
HEAD_DIM = 64
N_HEADS = D_MODEL // HEAD_DIM
D_DECAY_LORA = 64
D_AAA_LORA = 64
D_GATE_LORA = 128
GN_EPS = 64e-5
CONV_WIDTH = 31
LN_EPS = 1e-5
N_EXPERTS = 64
N_GROUPS = 8
TOPK_GROUPS = 4
TOP_K = 8
D_EXPERT = 256
ROUTED_SCALE = 2.5
RMS_EPS = 1e-6

O_K = D_MODEL
O_V = 2 * D_MODEL
O_W = 3 * D_MODEL
O_A = O_W + D_DECAY_LORA
O_G = O_A + D_AAA_LORA
C_RWKV = O_G + D_GATE_LORA
C_REST = 4 * D_MODEL

LANES = 128
SUBLANES = 8
CHUNK = 64
PAIR = 2 * HEAD_DIM
VMEM_LIMIT = 60 * 1024 * 1024

NN = ((1,), (0,))
NT = ((1,), (1,))
TN = ((0,), (0,))


def _dg(a, b, dims):
    return lax.dot_general(a, b, (dims, ((), ())), preferred_element_type=F32)


def _split2(x):
    hi = x.astype(BF16)
    lo = (x - hi.astype(F32)).astype(BF16)
    return hi, lo


def _split3(x):
    hi = x.astype(BF16)
    r1 = x - hi.astype(F32)
    mid = r1.astype(BF16)
    lo = (r1 - mid.astype(F32)).astype(BF16)
    return hi, mid, lo


def _mm1(a, b, dims=NN):
    return _dg(a.astype(BF16), b.astype(BF16), dims)


def _mm3(a, b, dims=NN):
    ah, al = _split2(a)
    bh, bl = _split2(b)
    m = a.shape[0]
    if dims[0] == (1,) and m % SUBLANES == 0:
        top = _dg(jnp.concatenate([ah, al], axis=0), bh, dims)
        return top[0:m] + (_dg(ah, bl, dims) + top[m:2 * m])
    return _dg(ah, bh, dims) + (_dg(ah, bl, dims) + _dg(al, bh, dims))


def _mm_exact_rhs(a, b_bf16, dims=NN, passes=3):
    if passes == 1:
        return _dg(a.astype(BF16), b_bf16, dims)
    if passes == 2:
        h, l = _split2(a)
        return _dg(h, b_bf16, dims) + _dg(l, b_bf16, dims)
    h, m, l = _split3(a)
    return _dg(h, b_bf16, dims) + (_dg(m, b_bf16, dims) + _dg(l, b_bf16, dims))


def _rmsnorm(x, g):
    return x * lax.rsqrt(jnp.mean(x * x, axis=-1, keepdims=True) + RMS_EPS) * g


def _sigmoid(x):
    return 1.0 / (1.0 + jnp.exp(-x))


def _seg_mats():
    row = lax.broadcasted_iota(jnp.int32, (D_MODEL, LANES), 0) // HEAD_DIM
    col = lax.broadcasted_iota(jnp.int32, (D_MODEL, LANES), 1)
    seg = (row == col).astype(BF16)
    rowt = lax.broadcasted_iota(jnp.int32, (LANES, D_MODEL), 0)
    colt = lax.broadcasted_iota(jnp.int32, (LANES, D_MODEL), 1) // HEAD_DIM
    exp = (rowt == colt).astype(BF16)
    return seg, exp


def _cparams(sem):
    return pltpu.CompilerParams(dimension_semantics=sem, vmem_limit_bytes=VMEM_LIMIT)


def _full(shape):
    n = len(shape)
    return pl.BlockSpec(shape, lambda *_: (0,) * n)


def _prep_math(zr, zp, mu, w0, wdu, a0, aup, gup, kk_w, ka_w, rk_w, seg, exp):
    zm = zr + (zp - zr) * mu
    r = zm[:, 0:O_K]
    k = zm[:, O_K:O_V]
    v = zm[:, O_V:O_W]
    xw = jnp.tanh(zm[:, O_W:O_A])
    xa = zm[:, O_A:O_G]
    xg = _sigmoid(zm[:, O_G:C_RWKV])
    y = -(w0 + _mm1(xw, wdu))
    softplus = jnp.maximum(y, 0.0) + jnp.log(1.0 + jnp.exp(-jnp.abs(y)))
    lw = -jnp.exp(-softplus - 0.5)
    a = _sigmoid(a0 + _mm1(xa, aup))
    g = _mm1(xg, gup)
    kkr = k * kk_w
    ss = _mm_exact_rhs(kkr * kkr, seg, passes=2)
    inv = 1.0 / jnp.maximum(jnp.sqrt(ss), 1e-12)
    kk = kkr * _mm_exact_rhs(inv, exp, passes=2)
    kf = k * (1.0 + (a - 1.0) * ka_w)
    b = kk * a
    rk = _mm_exact_rhs(r * kf * rk_w, seg, passes=1)
    bonus = _mm_exact_rhs(rk, exp, passes=1) * v
    return r, lw, kf, v, kk, b, g, bonus


def _prep_seq_body(zr, vec_refs, carry_ref):
    tm = zr.shape[0]
    rows = lax.broadcasted_iota(jnp.int32, zr.shape, 0)
    zp = jnp.where(rows == 0, carry_ref[...], pltpu.roll(zr, 1, 0))
    carry_ref[...] = zr[tm - 1:tm, :]
    seg, exp = _seg_mats()
    return _prep_math(zr, zp, *[ref[...] for ref in vec_refs], seg, exp)


def _prep_batch_kernel(x_ref, xp_ref, g1_ref, w_ref, mu_ref, w0_ref, wdu_ref, a0_ref, aup_ref, gup_ref,
                       kkw_ref, kaw_ref, rkw_ref,
                       r_ref, lw_ref, k_ref, v_ref, kk_ref, b_ref, g_ref, bo_ref, xn_ref):
    xn = _rmsnorm(x_ref[...], g1_ref[...])
    w = w_ref[...]
    zr = _dg(xn.astype(BF16), w, NN)
    zp = _dg(xp_ref[...].astype(BF16), w, NN)
    seg, exp = _seg_mats()
    outs = _prep_math(zr, zp, mu_ref[...], w0_ref[...], wdu_ref[...], a0_ref[...], aup_ref[...], gup_ref[...],
                      kkw_ref[...], kaw_ref[...], rkw_ref[...], seg, exp)
    for o_ref, val in zip((r_ref, lw_ref, k_ref, v_ref, kk_ref, b_ref, g_ref, bo_ref), outs):
        o_ref[...] = val
    xn_ref[...] = xn


def _prep_params(p):
    return (p['norm1_g'], p['w_rwkv'], p['mu_shift'], p['w0'], p['w_decay_up'], p['a0'], p['a_up'], p['g_up'],
            p['k_k'], p['k_a'], p['r_k'])


def _rwkv_prep_batch(x, xprev, p):
    N, D = x.shape
    params = _prep_params(p)
    out_shape = [jax.ShapeDtypeStruct((N, D), F32)] * 9
    return pl.pallas_call(
        _prep_batch_kernel,
        grid=(1,),
        in_specs=[_full(x.shape), _full(xprev.shape)] + [_full(a.shape) for a in params],
        out_specs=[_full((N, D))] * 9,
        out_shape=out_shape,
        compiler_params=_cparams(("arbitrary",)),
        name="rwkv_prep_batch",
    )(x, xprev, *params)


HIST = 32


def _conv_tail(c, cb, lg, lb, mix_a, mix_b):
    c = c + cb
    mean = jnp.mean(c, axis=-1, keepdims=True)
    d = c - mean
    var = jnp.mean(d * d, axis=-1, keepdims=True)
    y = d * lax.rsqrt(var + LN_EPS) * lg + lb
    out_b = y * _sigmoid(y)
    return _sigmoid(mix_a), _sigmoid(mix_b) * out_b


def _glu_mix(xn, w_ref):
    xb = xn.astype(BF16)
    glu_a = _dg(xb, w_ref[:, 0:D_MODEL], NN)
    glu_b = _dg(xb, w_ref[:, D_MODEL:2 * D_MODEL], NN)
    mix_a = _dg(xb, w_ref[:, 2 * D_MODEL:3 * D_MODEL], NN)
    mix_b = _dg(xb, w_ref[:, 3 * D_MODEL:4 * D_MODEL], NN)
    return glu_a * _sigmoid(glu_b), mix_a, mix_b


def _conv_taps(u, cw_ref, ubuf_ref):
    tm = u.shape[0]
    npast = CONV_WIDTH - 1
    ubuf_ref[pl.ds(HIST, tm), :] = u
    c = cw_ref[pl.ds(npast, 1), :] * ubuf_ref[pl.ds(HIST, tm), :]
    for s in range(SUBLANES):
        offs = [o for o in range(HIST - npast, HIST) if o % SUBLANES == s]
        grp = None
        for o in offs:
            term = cw_ref[pl.ds(o - (HIST - npast), 1), :] * ubuf_ref[pl.ds(o - s, tm + SUBLANES), :]
            grp = term if grp is None else grp + term
        c = c + grp[s:s + tm, :]
    return c


def _mix_seq_kernel(x_ref, zp0_ref, sc_ref, g1_ref, wr_ref, mu_ref, w0_ref, wdu_ref, a0_ref, aup_ref, gup_ref,
                    kkw_ref, kaw_ref, rkw_ref, wc_ref, cw_ref, cb_ref, lg_ref, lb_ref,
                    r_ref, lw_ref, k_ref, v_ref, kk_ref, b_ref, g_ref, bo_ref, xl_ref, ga_ref, bm_ref, so_ref,
                    carry_ref, ubuf_ref):
    t = pl.program_id(1)
    nt = pl.num_programs(1)
    tm = x_ref.shape[1]
    npast = CONV_WIDTH - 1

    @pl.when(t == 0)
    def _():
        carry_ref[...] = zp0_ref[0]
        ubuf_ref[pl.ds(0, HIST - npast), :] = jnp.zeros((HIST - npast, D_MODEL), F32)
        ubuf_ref[pl.ds(HIST - npast, npast), :] = sc_ref[0]

    @pl.when(t > 0)
    def _():
        ubuf_ref[pl.ds(0, HIST), :] = ubuf_ref[pl.ds(tm, HIST), :]

    xn = _rmsnorm(x_ref[0], g1_ref[...])
    xb = xn.astype(BF16)
    glu_a = _dg(xb, wc_ref[:, 0:D_MODEL], NN)
    glu_b = _dg(xb, wc_ref[:, D_MODEL:2 * D_MODEL], NN)
    c = _conv_taps(glu_a * _sigmoid(glu_b), cw_ref, ubuf_ref)
    zr = _dg(xb, wr_ref[...], NN)
    mix_a = _dg(xb, wc_ref[:, 2 * D_MODEL:3 * D_MODEL], NN)
    mix_b = _dg(xb, wc_ref[:, 3 * D_MODEL:4 * D_MODEL], NN)
    ga, bm = _conv_tail(c, cb_ref[...], lg_ref[...], lb_ref[...], mix_a, mix_b)
    outs = _prep_seq_body(zr, (mu_ref, w0_ref, wdu_ref, a0_ref, aup_ref, gup_ref, kkw_ref, kaw_ref, rkw_ref), carry_ref)
    for o_ref, val in zip((r_ref, lw_ref, k_ref, v_ref, kk_ref, b_ref, g_ref, bo_ref), outs):
        o_ref[0] = val
    xl_ref[0] = xn[tm - 1:tm, :]
    ga_ref[0] = ga
    bm_ref[0] = bm

    @pl.when(t == nt - 1)
    def _():
        so_ref[0] = ubuf_ref[pl.ds(tm + HIST - npast, npast), :]


def _conv_batch_kernel(x_ref, sc_ref, g1_ref, w_ref, cw_ref, cb_ref, lg_ref, lb_ref,
                       ga_ref, bm_ref, so_ref):
    npast = CONV_WIDTH - 1
    xn = _rmsnorm(x_ref[...], g1_ref[...])
    u, mix_a, mix_b = _glu_mix(xn, w_ref)
    c = cw_ref[pl.ds(npast, 1), :] * u
    for j in range(npast):
        c = c + cw_ref[pl.ds(j, 1), :] * sc_ref[j]
    ga, bm = _conv_tail(c, cb_ref[...], lg_ref[...], lb_ref[...], mix_a, mix_b)
    ga_ref[...] = ga
    bm_ref[...] = bm
    for j in range(npast - 1):
        so_ref[j] = sc_ref[j + 1]
    so_ref[npast - 1] = u


def _conv_params(p):
    return (p['norm1_g'], p['w_rest'], p['conv_w'], p['conv_b'], p['cln_g'], p['cln_b'])


def _resident(shape):
    n = len(shape)
    return pl.BlockSpec(shape, lambda *_: (0,) * n, pipeline_mode=pl.Buffered(1))


def _mix_seq(x, zp0, s_conv, p, tm):
    B, T, D = x.shape
    tm = min(tm, T)
    params = _prep_params(p) + _conv_params(p)[1:]
    npast = CONV_WIDTH - 1
    seq = pl.BlockSpec((1, tm, D), lambda b, t: (b, t, 0))
    per_b = lambda n: pl.BlockSpec((1, n, D), lambda b, t: (b, 0, 0))
    return pl.pallas_call(
        _mix_seq_kernel,
        grid=(B, T // tm),
        in_specs=[seq, pl.BlockSpec((1, 1, C_RWKV), lambda b, t: (b, 0, 0)), per_b(npast)]
                 + [_resident(a.shape) for a in params],
        out_specs=[seq] * 8 + [per_b(1), seq, seq, per_b(npast)],
        out_shape=[jax.ShapeDtypeStruct((B, T, D), F32)] * 8 + [jax.ShapeDtypeStruct((B, 1, D), F32)]
                  + [jax.ShapeDtypeStruct((B, T, D), F32)] * 2 + [jax.ShapeDtypeStruct((B, npast, D), F32)],
        scratch_shapes=[pltpu.VMEM((1, C_RWKV), F32), pltpu.VMEM((tm + HIST, D), F32)],
        compiler_params=_cparams(("parallel", "arbitrary")),
        name="mix_seq",
    )(x, zp0, s_conv, *params)


def _conv_branch_batch(x, s_conv_t, p):
    N, D = x.shape
    params = _conv_params(p)
    return pl.pallas_call(
        _conv_batch_kernel,
        grid=(1,),
        in_specs=[_full(x.shape), _full(s_conv_t.shape)] + [_full(a.shape) for a in params],
        out_specs=[_full((N, D)), _full((N, D)), _full(s_conv_t.shape)],
        out_shape=[jax.ShapeDtypeStruct((N, D), F32)] * 2 + [jax.ShapeDtypeStruct(s_conv_t.shape, F32)],
        compiler_params=_cparams(("arbitrary",)),
        name="conv_branch_batch",
    )(x, s_conv_t, *params)


def _pair_masks(shape):
    lane = lax.broadcasted_iota(jnp.int32, shape, 1)
    return lane < HEAD_DIM


def _bd(y, m0):
    zero = jnp.zeros_like(y)
    return jnp.concatenate([jnp.where(m0, y, zero), jnp.where(m0, zero, y)], axis=0)


def _bdmm(x, y, m0, mm):
    return mm(x, _bd(y, m0), NN)


def _bdmm_nt(x, y, m0, mm):
    return mm(x, _bd(y, m0), NT)


def _bdmm_tn(x, y, m0, mm):
    a = mm(x, y, TN)
    return jnp.where(m0, a[0:HEAD_DIM, :], a[HEAD_DIM:PAIR, :])


def _map(f, *lists):
    return [f(*xs) for xs in zip(*lists)]


def _chunk_pairs(r, cum, lw, k, v, kk, b, hooks=()):
    L = CHUNK
    hooks = list(hooks)

    def run_hook():
        if hooks:
            hooks.pop(0)()

    m0 = _pair_masks((L, PAIR))
    trow = lax.broadcasted_iota(jnp.int32, (L, PAIR), 0)
    icol = lax.broadcasted_iota(jnp.int32, (L, PAIR), 1) % HEAD_DIM
    strict = icol < trow
    incl = icol <= trow
    eye = (icol == trow).astype(F32)
    zero = jnp.zeros((L, PAIR), F32)

    cl = [c[L - 1:L, :] for c in cum]
    alpha = _map(lambda x, c, w: x * jnp.exp(c - w), kk, cum, lw)
    rho = _map(lambda x, c: x * jnp.exp(c), r, cum)
    einv = [jnp.exp(-c) for c in cum]
    kappa = _map(lambda x, e: x * e, k, einv)
    beta = _map(lambda x, e: x * e, b, einv)
    etail = _map(lambda c1, c: jnp.exp(c1 - c), cl, cum)
    kappa2 = _map(lambda x, e: x * e, k, etail)
    beta2 = _map(lambda x, e: x * e, b, etail)
    dl = [jnp.exp(c1) for c1 in cl]

    mm = _mm1

    def rows2(x0, x1, y, dims):
        out = mm(jnp.concatenate([x0, x1], axis=0), _bd(y, m0), dims)
        return out[0:L, :], out[L:2 * L, :]

    def cols2(x, y0, y1):
        out = mm(x, jnp.concatenate([_bd(y0, m0), _bd(y1, m0)], axis=1), NN)
        return out[:, 0:PAIR], out[:, PAIR:2 * PAIR]

    def scores(a, q, y0, y1):
        out = mm(jnp.concatenate([a, q], axis=0), jnp.concatenate([_bd(y0, m0), _bd(y1, m0)], axis=0), NT)
        return out[0:L, 0:PAIR], out[L:2 * L, 0:PAIR], out[0:L, PAIR:2 * PAIR], out[L:2 * L, PAIR:2 * PAIR]

    sc = _map(scores, alpha, rho, kappa, beta)
    m_k = [jnp.where(strict, x[0], zero) for x in sc]
    n_k = [jnp.where(incl, x[1], zero) for x in sc]
    m_b = [jnp.where(strict, x[2], zero) for x in sc]
    n_b = [jnp.where(incl, x[3], zero) for x in sc]
    run_hook()

    nn = [-m for m in m_b]
    tinv = [eye + q for q in nn]
    pw = _map(lambda q: _bdmm(q, q, m0, mm), nn)
    mvn = _map(lambda a, q, y: rows2(a, q, y, NN), m_k, n_k, v)
    for it in range(4):
        res = _map(lambda q, t: rows2(q, t, q, NN), pw, tinv)
        tinv = _map(lambda t, x: t + x[1], tinv, res)
        pw = [x[0] for x in res]
        if it % 2 == 1:
            run_hook()
    tinv = _map(lambda t, q: t + _bdmm(t, q, m0, mm), tinv, pw)

    mv = [x[0] for x in mvn]
    nkv = [x[1] for x in mvn]
    aw = _map(cols2, tinv, alpha, mv)
    alpha2 = [x[0] for x in aw]
    w = [x[1] for x in aw]
    run_hook()
    nb = _map(cols2, n_b, alpha2, w)
    rho2 = _map(lambda x, y: x - y[0], rho, nb)
    o2 = _map(lambda x, y: x - y[1], nkv, nb)

    def tn2(x0, x1, y):
        a = mm(jnp.concatenate([x0, x1], axis=1), y, TN)
        pick = lambda z: jnp.where(m0, z[0:HEAD_DIM, :], z[HEAD_DIM:PAIR, :])
        return pick(a[0:PAIR, :]), pick(a[PAIR:2 * PAIR, :])

    ab = _map(tn2, alpha2, w, beta2)
    g = _map(lambda d, x: eye * d - x[0], dl, ab)
    h = _map(lambda x, k2, y: _bdmm_tn(x, k2, m0, mm) - y[1], v, kappa2, ab)
    while hooks:
        run_hook()
    return rho2, o2, g, h


def _wkv_kernel(r_ref, lw_ref, k_ref, v_ref, kk_ref, b_ref, o_ref, s_out_ref, rho_s, o2_s, g_s, h_s, s_ref):
    j = pl.program_id(1)
    n = pl.num_programs(1)
    L = CHUNK
    rows, width = r_ref.shape[1], r_ref.shape[2]
    nsub = rows // L
    m0 = _pair_masks((L, PAIR))
    sls = [slice(p * PAIR, (p + 1) * PAIR) for p in range(width // PAIR)]

    @pl.when(j == 0)
    def _():
        for ref in (rho_s, o2_s, g_s, h_s, s_ref):
            ref[...] = jnp.zeros_like(ref)

    state = [s_ref[:, sl] for sl in sls]

    def recur(c, row0):
        rs = pl.ds(c * L, L)
        o = [_bdmm_nt(rho_s[rs, sl], x, m0, _mm1) + o2_s[rs, sl] for sl, x in zip(sls, state)]
        state[:] = [_bdmm(x, g_s[rs, sl], m0, _mm3) + h_s[rs, sl] for sl, x in zip(sls, state)]
        dst = pl.ds(pl.multiple_of(row0 + c * L, L), L)
        for sl, ov in zip(sls, o):
            o_ref[0, dst, sl] = ov

    prev_row0 = jnp.maximum(j - 1, 0) * rows
    hooks = [functools.partial(recur, c, prev_row0) for c in range(nsub)]

    items = [(pl.ds(c * L, L), sl) for c in range(nsub) for sl in sls]
    lw = [lw_ref[0, rs, sl] for rs, sl in items]
    trow = lax.broadcasted_iota(jnp.int32, (L, PAIR), 0)
    cum = lw
    shift = 1
    while shift < L:
        cum = [x + jnp.where(trow >= shift, pltpu.roll(x, shift, 0), 0.0) for x in cum]
        shift *= 2
    pick = lambda ref: [ref[0, rs, sl] for rs, sl in items]
    outs = _chunk_pairs(pick(r_ref), cum, lw, pick(k_ref), pick(v_ref), pick(kk_ref), pick(b_ref), hooks)
    for ref, vals in zip((rho_s, o2_s, g_s, h_s), outs):
        for (rs, sl), val in zip(items, vals):
            ref[rs, sl] = val
    for sl, sv in zip(sls, state):
        s_ref[:, sl] = sv

    @pl.when(j == n - 1)
    def _():
        state[:] = [s_ref[:, sl] for sl in sls]
        for c in range(nsub):
            recur(c, j * rows)
        for p, sv in enumerate(state):
            s_out_ref[0, 2 * p] = sv[:, :HEAD_DIM]
            s_out_ref[0, 2 * p + 1] = sv[:, HEAD_DIM:]


def _wkv(r, lw, k, v, kk, b, rows):
    B, T, D = r.shape
    blk = pl.BlockSpec((1, rows, D), lambda bi, c: (bi, c, 0))
    return pl.pallas_call(
        _wkv_kernel,
        grid=(B, T // rows),
        in_specs=[blk] * 6,
        out_specs=[pl.BlockSpec((1, T, D), lambda bi, c: (bi, 0, 0)),
                   pl.BlockSpec((1, N_HEADS, HEAD_DIM, HEAD_DIM), lambda bi, c: (bi, 0, 0, 0))],
        out_shape=[jax.ShapeDtypeStruct((B, T, D), F32),
                   jax.ShapeDtypeStruct((B, N_HEADS, HEAD_DIM, HEAD_DIM), F32)],
        scratch_shapes=[pltpu.VMEM((rows, D), F32)] * 4 + [pltpu.VMEM((HEAD_DIM, D), F32)],
        compiler_params=_cparams(("parallel", "arbitrary")),
        name="wkv",
    )(r, lw, k, v, kk, b)


def _wkv_step_kernel(s_ref, vec_ref, o_ref, so_ref):
    r, lw, k, v, kk, b = [vec_ref[i] for i in range(6)]
    d = jnp.exp(lw)
    for vi in range(HEAD_DIM):
        s = s_ref[0, vi]
        sa = -jnp.sum(s * kk, axis=0, keepdims=True)
        s_new = s * d + sa * b + v[vi:vi + 1, :] * k
        so_ref[0, vi] = s_new
        o_ref[pl.ds(vi, 1), :] = jnp.sum(s_new * r, axis=0, keepdims=True)


def _wkv_step(s_t, vecs_t):
    n = s_t.shape[-1]
    st = pl.BlockSpec((1, HEAD_DIM, HEAD_DIM, n), lambda h: (h, 0, 0, 0))
    return pl.pallas_call(
        _wkv_step_kernel,
        grid=(N_HEADS,),
        in_specs=[st, pl.BlockSpec((6, HEAD_DIM, n), lambda h: (0, h, 0))],
        out_specs=[pl.BlockSpec((HEAD_DIM, n), lambda h: (h, 0)), st],
        out_shape=[jax.ShapeDtypeStruct((D_MODEL, n), F32), jax.ShapeDtypeStruct(s_t.shape, F32)],
        compiler_params=_cparams(("parallel",)),
        name="wkv_step",
    )(s_t, vecs_t)


def _first_max(x, axis, n):
    m = jnp.max(x, axis=axis, keepdims=True)
    idx = lax.broadcasted_iota(jnp.int32, x.shape, axis)
    first = jnp.min(jnp.where(x == m, idx, n), axis=axis, keepdims=True)
    return m, idx == first


def _route(scores, biased):
    tm = scores.shape[1]
    per = N_EXPERTS // N_GROUPS
    neg = jnp.full((), -jnp.inf, F32)
    b3 = biased.reshape(N_GROUPS, per, tm)
    m1, hit = _first_max(b3, 1, per)
    m2 = jnp.max(jnp.where(hit, neg, b3), axis=1, keepdims=True)
    gs = (m1 + m2).reshape(N_GROUPS, tm)
    gsel = jnp.zeros((N_GROUPS, tm), jnp.bool_)
    for _ in range(TOPK_GROUPS):
        _, hit = _first_max(gs, 0, N_GROUPS)
        gsel = jnp.logical_or(gsel, hit)
        gs = jnp.where(hit, neg, gs)
    emask = jnp.broadcast_to(gsel.reshape(N_GROUPS, 1, tm), (N_GROUPS, per, tm)).reshape(N_EXPERTS, tm)
    cand = jnp.where(emask, biased, neg)
    esel = jnp.zeros((N_EXPERTS, tm), jnp.bool_)
    for _ in range(TOP_K):
        _, hit = _first_max(cand, 0, N_EXPERTS)
        esel = jnp.logical_or(esel, hit)
        cand = jnp.where(hit, neg, cand)
    wsel = jnp.where(esel, scores, 0.0)
    return wsel / jnp.sum(wsel, axis=0, keepdims=True) * ROUTED_SCALE


def _post_kernel(o_ref, g_ref, bo_ref, ga_ref, bm_ref, x_ref,
                 gng_ref, gnb_ref, wo_ref, n2_ref, wsg_ref, wsu_ref, wsd_ref, wrt_ref, eb_ref,
                 base_ref, hn_ref, gate_ref):
    seg, exp = _seg_mats()
    o = o_ref[...]
    inv_n = 1.0 / HEAD_DIM
    mean = _mm_exact_rhs(_mm_exact_rhs(o, seg, passes=1) * inv_n, exp, passes=2)
    d = o - mean
    var = _mm_exact_rhs(d * d, seg, passes=1) * inv_n
    rstd = _mm_exact_rhs(lax.rsqrt(var + GN_EPS), exp, passes=1)
    ogn = d * rstd * gng_ref[...] + gnb_ref[...]
    out_a = (ogn + bo_ref[...]) * g_ref[...]
    merged = ga_ref[...] * out_a + bm_ref[...]
    h = x_ref[...] + _dg(merged.astype(BF16), wo_ref[...], NN)
    hn = _rmsnorm(h, n2_ref[...])
    hb = hn.astype(BF16)
    sg = _dg(hb, wsg_ref[...], NN)
    su = _dg(hb, wsu_ref[...], NN)
    shared = _dg((sg * _sigmoid(sg) * su).astype(BF16), wsd_ref[...], NN)
    base_ref[...] = h + shared
    hn_ref[...] = hb
    logits = _mm3(wrt_ref[...], hn, NT)
    scores = _sigmoid(logits)
    gate_t = _route(scores, scores + eb_ref[...])
    tm = gate_t.shape[1]
    gate_pad = jnp.concatenate([gate_t, jnp.zeros((LANES - N_EXPERTS, tm), F32)], axis=0)
    gate_ref[...] = gate_pad.T


def _post(o, g, bonus, ga, bm, x, p, tm):
    N, D = x.shape
    tm = min(tm, N)
    params = (p['gn_g'], p['gn_b'], p['w_out'], p['norm2_g'], p['ws_gate'], p['ws_up'], p['ws_down'],
              p['w_router_t'], p['e_bias'])
    row = pl.BlockSpec((tm, D), lambda i: (i, 0))
    return pl.pallas_call(
        _post_kernel,
        grid=(N // tm,),
        in_specs=[row] * 6 + [_full(a.shape) for a in params],
        out_specs=[row, row, pl.BlockSpec((tm, LANES), lambda i: (i, 0))],
        out_shape=[jax.ShapeDtypeStruct((N, D), F32), jax.ShapeDtypeStruct((N, D), BF16),
                   jax.ShapeDtypeStruct((N, LANES), F32)],
        compiler_params=_cparams(("parallel",)),
        name="post",
    )(o, g, bonus, ga, bm, x, *params)


def _moe_kernel(x_ref, gate_ref, base_ref, wg_ref, wu_ref, wd_ref, nf_ref, y_ref):
    j = pl.program_id(1)
    nj = pl.num_programs(1)
    eps = wg_ref.shape[0]

    @pl.when(j == 0)
    def _():
        y_ref[...] = base_ref[...]

    gate = gate_ref[...]
    lane = lax.broadcasted_iota(jnp.int32, gate.shape, 1)
    x = x_ref[...]
    cols = [jnp.sum(jnp.where(lane == j * eps + q, gate, 0.0), axis=1, keepdims=True) for q in range(eps)]
    hg = [_dg(x, wg_ref[q], NN) for q in range(eps)]
    hu = [_dg(x, wu_ref[q], NN) for q in range(eps)]
    hh = [(hg[q] * _sigmoid(hg[q]) * hu[q] * cols[q]).astype(BF16) for q in range(eps)]
    y_ref[...] += _dg(jnp.concatenate(hh, axis=1), wd_ref[...].reshape(eps * D_EXPERT, D_MODEL), NN)

    @pl.when(j == nj - 1)
    def _():
        y_ref[...] = _rmsnorm(y_ref[...], nf_ref[...])


def _moe(hn, gate, base, p, tm, eps):
    N, D = base.shape
    tm = min(tm, N)
    row = lambda w: pl.BlockSpec((tm, w), lambda i, e: (i, 0))
    return pl.pallas_call(
        _moe_kernel,
        grid=(N // tm, N_EXPERTS // eps),
        in_specs=[row(D), row(LANES), row(D),
                  pl.BlockSpec((eps, D, D_EXPERT), lambda i, e: (e, 0, 0)),
                  pl.BlockSpec((eps, D, D_EXPERT), lambda i, e: (e, 0, 0)),
                  pl.BlockSpec((eps, D_EXPERT, D), lambda i, e: (e, 0, 0)),
                  pl.BlockSpec((1, D), lambda i, e: (0, 0))],
        out_specs=row(D),
        out_shape=jax.ShapeDtypeStruct((N, D), F32),
        compiler_params=_cparams(("parallel", "arbitrary")),
        name="moe",
    )(hn, gate, base, p['w_gate'], p['w_up'], p['w_down'], p['normf_g'])


def kernel(x_prompt, x_sample, state_wkv, state_shift, state_conv, norm1_g, w_in, mu_shift, w0, w_decay_up, a0, a_up, g_up, k_k, k_a, r_k, gn_g, gn_b, conv_w, conv_b, cln_g, cln_b, w_out, norm2_g, w_router, e_bias, w_gate, w_up, w_down, ws_gate, ws_up, ws_down, normf_g):
    depth = w_in.shape[0]
    assert depth == 1
    B, T, D = x_prompt.shape
    NS = x_sample.shape[0]
    assert x_sample.shape[1] == 1 and D == D_MODEL and T % CHUNK == 0
    row = lambda a: a[0].reshape(1, -1)
    p = {
        'norm1_g': row(norm1_g), 'mu_shift': row(mu_shift), 'w0': row(w0), 'a0': row(a0),
        'k_k': row(k_k), 'k_a': row(k_a), 'r_k': row(r_k), 'gn_g': row(gn_g), 'gn_b': row(gn_b),
        'conv_b': row(conv_b), 'cln_g': row(cln_g), 'cln_b': row(cln_b), 'norm2_g': row(norm2_g),
        'normf_g': normf_g.reshape(1, -1),
        'w_rwkv': w_in[0, :, :C_RWKV].astype(BF16), 'w_rest': w_in[0, :, C_RWKV:].astype(BF16),
        'w_decay_up': w_decay_up[0], 'a_up': a_up[0], 'g_up': g_up[0], 'conv_w': conv_w[0],
        'w_out': w_out[0].astype(BF16),
        'ws_gate': ws_gate[0].astype(BF16), 'ws_up': ws_up[0].astype(BF16), 'ws_down': ws_down[0].astype(BF16),
        'w_router_t': w_router[0].T, 'e_bias': e_bias[0].reshape(-1, 1),
        'w_gate': w_gate[0].astype(BF16), 'w_up': w_up[0].astype(BF16), 'w_down': w_down[0].astype(BF16),
    }

    zp0 = jnp.zeros((B, 1, C_RWKV), F32)
    r, lw, k, v, kk, b, g, bonus, shift_p, ga, bm, conv_p = _mix_seq(
        x_prompt, zp0, jnp.zeros((B, CONV_WIDTH - 1, D), F32), p, tm=256)
    o, wkv_p = _wkv(r, lw, k, v, kk, b, rows=4 * CHUNK)
    flat = lambda a: a.reshape(B * T, D)
    base, hn, gate = _post(flat(o), flat(g), flat(bonus), flat(ga), flat(bm), flat(x_prompt), p, tm=512)
    y_prompt = _moe(hn, gate, base, p, tm=1024, eps=8).reshape(B, T, D)

    xs = x_sample.reshape(NS, D)
    r, lw, k, v, kk, b, g, bonus, shift_s = _rwkv_prep_batch(xs, state_shift[0], p)
    ga, bm, conv_s_t = _conv_branch_batch(xs, jnp.swapaxes(state_conv[0], 0, 1), p)
    o_t, wkv_s_t = _wkv_step(jnp.transpose(state_wkv[0], (1, 2, 3, 0)),
                             jnp.transpose(jnp.stack([r, lw, k, v, kk, b]), (0, 2, 1)))
    o = o_t.T
    wkv_s = jnp.transpose(wkv_s_t, (3, 0, 1, 2))
    base, hn, gate = _post(o, g, bonus, ga, bm, xs, p, tm=128)
    y_sample = _moe(hn, gate, base, p, tm=128, eps=4).reshape(NS, 1, D)

    return (y_prompt, y_sample, wkv_p[None], shift_p.reshape(1, B, D), conv_p[None],
            wkv_s[None], shift_s[None], jnp.swapaxes(conv_s_t, 0, 1)[None])
```

```python
import functools

import jax
import jax.numpy as jnp
from jax import lax
from jax.experimental import pallas as pl
from jax.experimental.pallas import tpu as pltpu

F32 = jnp.float32
BF16 = jnp.bfloat16

D_MODEL = 1024
HEAD_DIM = 64
N_HEADS = D_MODEL // HEAD_DIM
D_DECAY_LORA = 64
D_AAA_LORA = 64
D_GATE_LORA = 128
GN_EPS = 64e-5
CONV_WIDTH = 31
LN_EPS = 1e-5
N_EXPERTS = 64
N_GROUPS = 8
TOPK_GROUPS = 4
TOP_K = 8
D_EXPERT = 256
ROUTED_SCALE = 2.5
RMS_EPS = 1e-6
DECAY_SCALE = 0.6065306597126334

O_K = D_MODEL
O_V = 2 * D_MODEL
O_W = 3 * D_MODEL
O_A = O_W + D_DECAY_LORA
O_G = O_A + D_AAA_LORA
C_RWKV = O_G + D_GATE_LORA
C_REST = 4 * D_MODEL

LANES = 128
SUBLANES = 8
CHUNK = 64
HEADS_PER_GROUP = 2
GROUP = HEADS_PER_GROUP * HEAD_DIM
VMEM_LIMIT = 60 * 1024 * 1024

NN = ((1,), (0,))
NT = ((1,), (1,))
TN = ((0,), (0,))


def _dg(a, b, dims):
    return lax.dot_general(a, b, (dims, ((), ())), preferred_element_type=F32)


def _split2(x):
    hi = x.astype(BF16)
    lo = (x - hi.astype(F32)).astype(BF16)
    return hi, lo


def _split3(x):
    hi = x.astype(BF16)
    r1 = x - hi.astype(F32)
    mid = r1.astype(BF16)
    lo = (r1 - mid.astype(F32)).astype(BF16)
    return hi, mid, lo


def _mm1(a, b, dims=NN):
    return _dg(a.astype(BF16), b.astype(BF16), dims)


def _mm3(a, b, dims=NN):
    ah, al = _split2(a)
    bh, bl = _split2(b)
    m = a.shape[0]
    if dims[0] == (1,) and m % SUBLANES == 0:
        top = _dg(jnp.concatenate([ah, al], axis=0), bh, dims)
        return top[0:m] + (_dg(ah, bl, dims) + top[m:2 * m])
    return _dg(ah, bh, dims) + (_dg(ah, bl, dims) + _dg(al, bh, dims))


def _mm_exact_rhs(a, b_bf16, dims=NN, passes=3):
    if passes == 1:
        return _dg(a.astype(BF16), b_bf16, dims)
    if passes == 2:
        h, l = _split2(a)
        return _dg(h, b_bf16, dims) + _dg(l, b_bf16, dims)
    h, m, l = _split3(a)
    return _dg(h, b_bf16, dims) + (_dg(m, b_bf16, dims) + _dg(l, b_bf16, dims))


def _rmsnorm(x, g):
    return x * lax.rsqrt(jnp.mean(x * x, axis=-1, keepdims=True) + RMS_EPS) * g


def _sigmoid(x):
    return 1.0 / (1.0 + jnp.exp(-x))


def _seg_mats():
    row = lax.broadcasted_iota(jnp.int32, (D_MODEL, LANES), 0) // HEAD_DIM
    col = lax.broadcasted_iota(jnp.int32, (D_MODEL, LANES), 1)
    seg = (row == col).astype(BF16)
    rowt = lax.broadcasted_iota(jnp.int32, (LANES, D_MODEL), 0)
    colt = lax.broadcasted_iota(jnp.int32, (LANES, D_MODEL), 1) // HEAD_DIM
    exp = (rowt == colt).astype(BF16)
    return seg, exp


def _cparams(sem):
    return pltpu.CompilerParams(dimension_semantics=sem, vmem_limit_bytes=VMEM_LIMIT)


def _full(shape):
    n = len(shape)
    return pl.BlockSpec(shape, lambda *_: (0,) * n)


def _prep_math(zr, zp, mu, w0, wdu, a0, aup, gup, kk_w, ka_w, rk_w, seg, exp):
    zm = zr + (zp - zr) * mu
    r = zm[:, 0:O_K]
    k = zm[:, O_K:O_V]
    v = zm[:, O_V:O_W]
    xw = jnp.tanh(zm[:, O_W:O_A])
    xa = zm[:, O_A:O_G]
    xg = _sigmoid(zm[:, O_G:C_RWKV])
    lw = -DECAY_SCALE * _sigmoid(w0 + _mm1(xw, wdu))
    a = _sigmoid(a0 + _mm1(xa, aup))
    g = _mm1(xg, gup)
    kkr = k * kk_w
    ss = _mm_exact_rhs(kkr * kkr, seg, passes=1)
    inv = 1.0 / jnp.maximum(jnp.sqrt(ss), 1e-12)
    kk = kkr * _mm_exact_rhs(inv, exp, passes=2)
    kf = k * (1.0 + (a - 1.0) * ka_w)
    b = kk * a
    rk = _mm_exact_rhs(r * kf * rk_w, seg, passes=1)
    bonus = _mm_exact_rhs(rk, exp, passes=1) * v
    return r, lw, kf, v, kk, b, g, bonus


def _prep_seq_body(zr, vec_refs, carry_ref):
    tm = zr.shape[0]
    rolled = pltpu.roll(zr, 1, 0)
    first = lax.broadcasted_iota(jnp.int32, (SUBLANES, zr.shape[1]), 0) == 0
    zp = jnp.concatenate([jnp.where(first, carry_ref[...], rolled[0:SUBLANES, :]), rolled[SUBLANES:, :]], axis=0)
    carry_ref[...] = zr[tm - 1:tm, :]
    seg, exp = _seg_mats()
    return _prep_math(zr, zp, *[ref[...] for ref in vec_refs], seg, exp)


def _prep_batch_kernel(x_ref, xp_ref, g1_ref, w_ref, mu_ref, w0_ref, wdu_ref, a0_ref, aup_ref, gup_ref,
                       kkw_ref, kaw_ref, rkw_ref,
                       r_ref, lw_ref, k_ref, v_ref, kk_ref, b_ref, g_ref, bo_ref, xn_ref):
    xn = _rmsnorm(x_ref[...], g1_ref[...])
    w = w_ref[...]
    zr = _dg(xn.astype(BF16), w, NN)
    zp = _dg(xp_ref[...].astype(BF16), w, NN)
    seg, exp = _seg_mats()
    outs = _prep_math(zr, zp, mu_ref[...], w0_ref[...], wdu_ref[...], a0_ref[...], aup_ref[...], gup_ref[...],
                      kkw_ref[...], kaw_ref[...], rkw_ref[...], seg, exp)
    for o_ref, val in zip((r_ref, lw_ref, k_ref, v_ref, kk_ref, b_ref, g_ref, bo_ref), outs):
        o_ref[...] = val
    xn_ref[...] = xn


def _prep_params(p):
    return (p['norm1_g'], p['w_rwkv'], p['mu_shift'], p['w0'], p['w_decay_up'], p['a0'], p['a_up'], p['g_up'],
            p['k_k'], p['k_a'], p['r_k'])


def _rwkv_prep_batch(x, xprev, p):
    N, D = x.shape
    params = _prep_params(p)
    out_shape = [jax.ShapeDtypeStruct((N, D), F32)] * 9
    return pl.pallas_call(
        _prep_batch_kernel,
        grid=(1,),
        in_specs=[_full(x.shape), _full(xprev.shape)] + [_full(a.shape) for a in params],
        out_specs=[_full((N, D))] * 9,
        out_shape=out_shape,
        compiler_params=_cparams(("arbitrary",)),
        name="rwkv_prep_batch",
    )(x, xprev, *params)


HIST = 32


def _conv_tail(c, cb, lg, lb, mix_a, mix_b):
    c = c + cb
    mean = jnp.mean(c, axis=-1, keepdims=True)
    d = c - mean
    var = jnp.mean(d * d, axis=-1, keepdims=True)
    y = d * lax.rsqrt(var + LN_EPS) * lg + lb
    out_b = y * _sigmoid(y)
    return _sigmoid(mix_a), _sigmoid(mix_b) * out_b


def _glu_mix(xn, w_ref):
    xb = xn.astype(BF16)
    glu_a = _dg(xb, w_ref[:, 0:D_MODEL], NN)
    glu_b = _dg(xb, w_ref[:, D_MODEL:2 * D_MODEL], NN)
    mix_a = _dg(xb, w_ref[:, 2 * D_MODEL:3 * D_MODEL], NN)
    mix_b = _dg(xb, w_ref[:, 3 * D_MODEL:4 * D_MODEL], NN)
    return glu_a * _sigmoid(glu_b), mix_a, mix_b


def _conv_taps(u, cw_ref, ubuf_ref):
    tm = u.shape[0]
    npast = CONV_WIDTH - 1
    ubuf_ref[pl.ds(HIST, tm), :] = u
    c = cw_ref[pl.ds(npast, 1), :] * ubuf_ref[pl.ds(HIST, tm), :]
    for s in range(SUBLANES):
        offs = [o for o in range(HIST - npast, HIST) if o % SUBLANES == s]
        grp = None
        for o in offs:
            term = cw_ref[pl.ds(o - (HIST - npast), 1), :] * ubuf_ref[pl.ds(o - s, tm + SUBLANES), :]
            grp = term if grp is None else grp + term
        c = c + grp[s:s + tm, :]
    return c


def _mix_seq_kernel(x_ref, zp0_ref, sc_ref, g1_ref, wr_ref, mu_ref, w0_ref, wdu_ref, a0_ref, aup_ref, gup_ref,
                    kkw_ref, kaw_ref, rkw_ref, wc_ref, cw_ref, cb_ref, lg_ref, lb_ref,
                    r_ref, lw_ref, k_ref, v_ref, kk_ref, b_ref, g_ref, bo_ref, xl_ref, ga_ref, bm_ref, so_ref,
                    carry_ref, ubuf_ref):
    t = pl.program_id(1)
    nt = pl.num_programs(1)
    tm = x_ref.shape[1]
    npast = CONV_WIDTH - 1

    @pl.when(t == 0)
    def _():
        carry_ref[...] = zp0_ref[0]
        ubuf_ref[pl.ds(0, HIST - npast), :] = jnp.zeros((HIST - npast, D_MODEL), F32)
        ubuf_ref[pl.ds(HIST - npast, npast), :] = sc_ref[0]

    @pl.when(t > 0)
    def _():
        ubuf_ref[pl.ds(0, HIST), :] = ubuf_ref[pl.ds(tm, HIST), :]

    xn = _rmsnorm(x_ref[0], g1_ref[...])
    xb = xn.astype(BF16)
    glu_a = _dg(xb, wc_ref[:, 0:D_MODEL], NN)
    glu_b = _dg(xb, wc_ref[:, D_MODEL:2 * D_MODEL], NN)
    c = _conv_taps(glu_a * _sigmoid(glu_b), cw_ref, ubuf_ref)
    zr = _dg(xb, wr_ref[...], NN)
    mix_a = _dg(xb, wc_ref[:, 2 * D_MODEL:3 * D_MODEL], NN)
    mix_b = _dg(xb, wc_ref[:, 3 * D_MODEL:4 * D_MODEL], NN)
    ga, bm = _conv_tail(c, cb_ref[...], lg_ref[...], lb_ref[...], mix_a, mix_b)
    outs = _prep_seq_body(zr, (mu_ref, w0_ref, wdu_ref, a0_ref, aup_ref, gup_ref, kkw_ref, kaw_ref, rkw_ref), carry_ref)
    for o_ref, val in zip((r_ref, lw_ref, k_ref, v_ref, kk_ref, b_ref, g_ref, bo_ref), outs):
        o_ref[0] = val
    xl_ref[0] = xn[tm - 1:tm, :]
    ga_ref[0] = ga
    bm_ref[0] = bm

    @pl.when(t == nt - 1)
    def _():
        so_ref[0] = ubuf_ref[pl.ds(tm + HIST - npast, npast), :]


def _conv_batch_kernel(x_ref, sc_ref, g1_ref, w_ref, cw_ref, cb_ref, lg_ref, lb_ref,
                       ga_ref, bm_ref, so_ref):
    npast = CONV_WIDTH - 1
    xn = _rmsnorm(x_ref[...], g1_ref[...])
    u, mix_a, mix_b = _glu_mix(xn, w_ref)
    c = cw_ref[pl.ds(npast, 1), :] * u
    for j in range(npast):
        c = c + cw_ref[pl.ds(j, 1), :] * sc_ref[j]
    ga, bm = _conv_tail(c, cb_ref[...], lg_ref[...], lb_ref[...], mix_a, mix_b)
    ga_ref[...] = ga
    bm_ref[...] = bm
    for j in range(npast - 1):
        so_ref[j] = sc_ref[j + 1]
    so_ref[npast - 1] = u


def _conv_params(p):
    return (p['norm1_g'], p['w_rest'], p['conv_w'], p['conv_b'], p['cln_g'], p['cln_b'])


def _resident(shape):
    n = len(shape)
    return pl.BlockSpec(shape, lambda *_: (0,) * n, pipeline_mode=pl.Buffered(1))


def _mix_seq(x, zp0, s_conv, p, tm):
    B, T, D = x.shape
    tm = min(tm, T)
    params = _prep_params(p) + _conv_params(p)[1:]
    npast = CONV_WIDTH - 1
    seq = pl.BlockSpec((1, tm, D), lambda b, t: (b, t, 0))
    per_b = lambda n: pl.BlockSpec((1, n, D), lambda b, t: (b, 0, 0))
    return pl.pallas_call(
        _mix_seq_kernel,
        grid=(B, T // tm),
        in_specs=[seq, pl.BlockSpec((1, 1, C_RWKV), lambda b, t: (b, 0, 0)), per_b(npast)]
                 + [_resident(a.shape) for a in params],
        out_specs=[seq] * 8 + [per_b(1), seq, seq, per_b(npast)],
        out_shape=[jax.ShapeDtypeStruct((B, T, D), F32)] * 8 + [jax.ShapeDtypeStruct((B, 1, D), F32)]
                  + [jax.ShapeDtypeStruct((B, T, D), F32)] * 2 + [jax.ShapeDtypeStruct((B, npast, D), F32)],
        scratch_shapes=[pltpu.VMEM((1, C_RWKV), F32), pltpu.VMEM((tm + HIST, D), F32)],
        compiler_params=_cparams(("parallel", "arbitrary")),
        name="mix_seq",
    )(x, zp0, s_conv, *params)


def _conv_branch_batch(x, s_conv_t, p):
    N, D = x.shape
    params = _conv_params(p)
    return pl.pallas_call(
        _conv_batch_kernel,
        grid=(1,),
        in_specs=[_full(x.shape), _full(s_conv_t.shape)] + [_full(a.shape) for a in params],
        out_specs=[_full((N, D)), _full((N, D)), _full(s_conv_t.shape)],
        out_shape=[jax.ShapeDtypeStruct((N, D), F32)] * 2 + [jax.ShapeDtypeStruct(s_conv_t.shape, F32)],
        compiler_params=_cparams(("arbitrary",)),
        name="conv_branch_batch",
    )(x, s_conv_t, *params)


def _head_ids(shape):
    return lax.broadcasted_iota(jnp.int32, shape, 1) // HEAD_DIM


def _bd(y, hid):
    zero = jnp.zeros_like(y)
    return jnp.concatenate([jnp.where(hid == h, y, zero) for h in range(HEADS_PER_GROUP)], axis=0)


def _diag_blocks(a, hid):
    out = a[(HEADS_PER_GROUP - 1) * HEAD_DIM:GROUP, :]
    for h in reversed(range(HEADS_PER_GROUP - 1)):
        out = jnp.where(hid == h, a[h * HEAD_DIM:(h + 1) * HEAD_DIM, :], out)
    return out


def _bdmm(x, y, hid, mm):
    return mm(x, _bd(y, hid), NN)


def _bdmm_nt(x, y, hid, mm):
    return mm(x, _bd(y, hid), NT)


def _bdmm_tn(x, y, hid, mm):
    return _diag_blocks(mm(x, y, TN), hid)


def _map(f, *lists):
    return [f(*xs) for xs in zip(*lists)]


def _chunk_groups(r, cum, lw, k, v, kk, b, hooks=()):
    L = CHUNK
    hooks = list(hooks)
    n_hooks, n_points, point = len(hooks), 8, [0]

    def run_hook():
        i = point[0]
        point[0] += 1
        if (i * n_hooks) // n_points != ((i + 1) * n_hooks) // n_points:
            hooks.pop(0)()

    hid = _head_ids((L, GROUP))
    trow = lax.broadcasted_iota(jnp.int32, (L, GROUP), 0)
    icol = lax.broadcasted_iota(jnp.int32, (L, GROUP), 1) % HEAD_DIM
    strict = icol < trow
    incl = icol <= trow
    eye = (icol == trow).astype(F32)
    zero = jnp.zeros((L, GROUP), F32)

    cl = [c[L - 1:L, :] for c in cum]
    alpha = _map(lambda x, c, w: x * jnp.exp(c - w), kk, cum, lw)
    rho = _map(lambda x, c: x * jnp.exp(c), r, cum)
    einv = [jnp.exp(-c) for c in cum]
    kappa = _map(lambda x, e: x * e, k, einv)
    beta = _map(lambda x, e: x * e, b, einv)
    etail = _map(lambda c1, c: jnp.exp(c1 - c), cl, cum)
    kappa2 = _map(lambda x, e: x * e, k, etail)
    beta2 = _map(lambda x, e: x * e, b, etail)
    dl = [jnp.exp(c1) for c1 in cl]

    mm = _mm1

    def rows2(x0, x1, y, dims):
        out = mm(jnp.concatenate([x0, x1], axis=0), _bd(y, hid), dims)
        return out[0:L, :], out[L:2 * L, :]

    def cols2(x, y0, y1):
        out = mm(x, jnp.concatenate([_bd(y0, hid), _bd(y1, hid)], axis=1), NN)
        return out[:, 0:GROUP], out[:, GROUP:2 * GROUP]

    def scores(a, q, y0, y1):
        out = mm(jnp.concatenate([a, q], axis=0), jnp.concatenate([_bd(y0, hid), _bd(y1, hid)], axis=0), NT)
        return out[0:L, 0:GROUP], out[L:2 * L, 0:GROUP], out[0:L, GROUP:2 * GROUP], out[L:2 * L, GROUP:2 * GROUP]

    sc = _map(scores, alpha, rho, kappa, beta)
    m_k = [jnp.where(strict, x[0], zero) for x in sc]
    n_k = [jnp.where(incl, x[1], zero) for x in sc]
    m_b = [jnp.where(strict, x[2], zero) for x in sc]
    n_b = [jnp.where(incl, x[3], zero) for x in sc]
    run_hook()

    nn = [-m for m in m_b]
    tinv = [eye + q for q in nn]
    pw = _map(lambda q: _bdmm(q, q, hid, mm), nn)
    mvn = _map(lambda a, q, y: rows2(a, q, y, NN), m_k, n_k, v)
    for it in range(4):
        res = _map(lambda q, t: rows2(q, t, q, NN), pw, tinv)
        tinv = _map(lambda t, x: t + x[1], tinv, res)
        pw = [x[0] for x in res]
        run_hook()
    tinv = _map(lambda t, q: t + _bdmm(t, q, hid, mm), tinv, pw)
    run_hook()

    mv = [x[0] for x in mvn]
    nkv = [x[1] for x in mvn]
    aw = _map(cols2, tinv, alpha, mv)
    alpha2 = [x[0] for x in aw]
    w = [x[1] for x in aw]
    run_hook()
    nb = _map(cols2, n_b, alpha2, w)
    rho2 = _map(lambda x, y: x - y[0], rho, nb)
    o2 = _map(lambda x, y: x - y[1], nkv, nb)
    run_hook()

    def tn2(x0, x1, y):
        a = mm(jnp.concatenate([x0, x1], axis=1), y, TN)
        return _diag_blocks(a[0:GROUP, :], hid), _diag_blocks(a[GROUP:2 * GROUP, :], hid)

    ab = _map(tn2, alpha2, w, beta2)
    g = _map(lambda d, x: eye * d - x[0], dl, ab)
    h = _map(lambda x, k2, y: _bdmm_tn(x, k2, hid, mm) - y[1], v, kappa2, ab)
    while hooks:
        hooks.pop(0)()
    return rho2, o2, g, h


def _wkv_kernel(r_ref, lw_ref, k_ref, v_ref, kk_ref, b_ref, o_ref, s_out_ref, rho_s, o2_s, g_s, h_s, s_ref):
    j = pl.program_id(1)
    n = pl.num_programs(1)
    L = CHUNK
    rows, width = r_ref.shape[1], r_ref.shape[2]
    nsub = rows // L
    hid = _head_ids((L, GROUP))
    sls = [slice(p * GROUP, (p + 1) * GROUP) for p in range(width // GROUP)]

    @pl.when(j == 0)
    def _():
        for ref in (rho_s, o2_s, g_s, h_s, s_ref):
            ref[...] = jnp.zeros_like(ref)

    state = [s_ref[:, sl] for sl in sls]

    def recur(c, row0):
        rs = pl.ds(c * L, L)
        o = [_bdmm_nt(rho_s[rs, sl], x, hid, _mm1) + o2_s[rs, sl] for sl, x in zip(sls, state)]
        state[:] = [_bdmm(x, g_s[rs, sl], hid, _mm3) + h_s[rs, sl] for sl, x in zip(sls, state)]
        dst = pl.ds(pl.multiple_of(row0 + c * L, L), L)
        for sl, ov in zip(sls, o):
            o_ref[0, dst, sl] = ov

    prev_row0 = jnp.maximum(j - 1, 0) * rows
    hooks = [functools.partial(recur, c, prev_row0) for c in range(nsub)]

    items = [(pl.ds(c * L, L), sl) for c in range(nsub) for sl in sls]
    lw = [lw_ref[0, rs, sl] for rs, sl in items]
    trow = lax.broadcasted_iota(jnp.int32, (L, GROUP), 0)
    cum = lw
    shift = 1
    while shift < L:
        cum = [x + jnp.where(trow >= shift, pltpu.roll(x, shift, 0), 0.0) for x in cum]
        shift *= 2
    pick = lambda ref: [ref[0, rs, sl] for rs, sl in items]
    outs = _chunk_groups(pick(r_ref), cum, lw, pick(k_ref), pick(v_ref), pick(kk_ref), pick(b_ref), hooks)
    for ref, vals in zip((rho_s, o2_s, g_s, h_s), outs):
        for (rs, sl), val in zip(items, vals):
            ref[rs, sl] = val
    for sl, sv in zip(sls, state):
        s_ref[:, sl] = sv

    @pl.when(j == n - 1)
    def _():
        state[:] = [s_ref[:, sl] for sl in sls]
        for c in range(nsub):
            recur(c, j * rows)
        for p, sv in enumerate(state):
            for h in range(HEADS_PER_GROUP):
                s_out_ref[0, HEADS_PER_GROUP * p + h] = sv[:, h * HEAD_DIM:(h + 1) * HEAD_DIM]


def _wkv(r, lw, k, v, kk, b, rows):
    B, T, D = r.shape
    blk = pl.BlockSpec((1, rows, D), lambda bi, c: (bi, c, 0))
    return pl.pallas_call(
        _wkv_kernel,
        grid=(B, T // rows),
        in_specs=[blk] * 6,
        out_specs=[pl.BlockSpec((1, T, D), lambda bi, c: (bi, 0, 0)),
                   pl.BlockSpec((1, N_HEADS, HEAD_DIM, HEAD_DIM), lambda bi, c: (bi, 0, 0, 0))],
        out_shape=[jax.ShapeDtypeStruct((B, T, D), F32),
                   jax.ShapeDtypeStruct((B, N_HEADS, HEAD_DIM, HEAD_DIM), F32)],
        scratch_shapes=[pltpu.VMEM((rows, D), F32)] * 4 + [pltpu.VMEM((HEAD_DIM, D), F32)],
        compiler_params=_cparams(("parallel", "arbitrary")),
        name="wkv",
    )(r, lw, k, v, kk, b)


def _wkv_step_kernel(s_ref, vec_ref, o_ref, so_ref):
    r, lw, k, v, kk, b = [vec_ref[i] for i in range(6)]
    d = jnp.exp(lw)
    for vi in range(HEAD_DIM):
        s = s_ref[0, vi]
        sa = -jnp.sum(s * kk, axis=0, keepdims=True)
        s_new = s * d + sa * b + v[vi:vi + 1, :] * k
        so_ref[0, vi] = s_new
        o_ref[pl.ds(vi, 1), :] = jnp.sum(s_new * r, axis=0, keepdims=True)


def _wkv_step(s_t, vecs_t):
    n = s_t.shape[-1]
    st = pl.BlockSpec((1, HEAD_DIM, HEAD_DIM, n), lambda h: (h, 0, 0, 0))
    return pl.pallas_call(
        _wkv_step_kernel,
        grid=(N_HEADS,),
        in_specs=[st, pl.BlockSpec((6, HEAD_DIM, n), lambda h: (0, h, 0))],
        out_specs=[pl.BlockSpec((HEAD_DIM, n), lambda h: (h, 0)), st],
        out_shape=[jax.ShapeDtypeStruct((D_MODEL, n), F32), jax.ShapeDtypeStruct(s_t.shape, F32)],
        compiler_params=_cparams(("parallel",)),
        name="wkv_step",
    )(s_t, vecs_t)


def _first_max(x, axis, n):
    m = jnp.max(x, axis=axis, keepdims=True)
    idx = lax.broadcasted_iota(jnp.int32, x.shape, axis)
    first = jnp.min(jnp.where(x == m, idx, n), axis=axis, keepdims=True)
    return m, idx == first


def _route(scores, biased):
    tm = scores.shape[1]
    per = N_EXPERTS // N_GROUPS
    neg = jnp.full((), -jnp.inf, F32)
    b3 = biased.reshape(N_GROUPS, per, tm)
    m1, hit = _first_max(b3, 1, per)
    m2 = jnp.max(jnp.where(hit, neg, b3), axis=1, keepdims=True)
    gs = (m1 + m2).reshape(N_GROUPS, tm)
    gsel = jnp.zeros((N_GROUPS, tm), jnp.bool_)
    for _ in range(TOPK_GROUPS):
        _, hit = _first_max(gs, 0, N_GROUPS)
        gsel = jnp.logical_or(gsel, hit)
        gs = jnp.where(hit, neg, gs)
    emask = jnp.broadcast_to(gsel.reshape(N_GROUPS, 1, tm), (N_GROUPS, per, tm)).reshape(N_EXPERTS, tm)
    cand = jnp.where(emask, biased, neg)
    esel = jnp.zeros((N_EXPERTS, tm), jnp.bool_)
    for _ in range(TOP_K):
        _, hit = _first_max(cand, 0, N_EXPERTS)
        esel = jnp.logical_or(esel, hit)
        cand = jnp.where(hit, neg, cand)
    wsel = jnp.where(esel, scores, 0.0)
    return wsel / jnp.sum(wsel, axis=0, keepdims=True) * ROUTED_SCALE


def _post_kernel(o_ref, g_ref, bo_ref, ga_ref, bm_ref, x_ref,
                 gng_ref, gnb_ref, wo_ref, n2_ref, wsg_ref, wsu_ref, wsd_ref, wrt_ref, eb_ref,
                 base_ref, hn_ref, gate_ref):
    seg, exp = _seg_mats()
    tm = o_ref.shape[0]
    nparts = 2 if tm % (2 * LANES) == 0 else 1
    rows = [pl.ds(i * (tm // nparts), tm // nparts) for i in range(nparts)]
    inv_n = 1.0 / HEAD_DIM
    o = [o_ref[rs, :] for rs in rows]
    mean = [_mm_exact_rhs(_mm_exact_rhs(x, seg, passes=1) * inv_n, exp, passes=2) for x in o]
    d = _map(lambda x, m: x - m, o, mean)
    var = [_mm_exact_rhs(x * x, seg, passes=1) * inv_n for x in d]
    rstd = [_mm_exact_rhs(lax.rsqrt(x + GN_EPS), exp, passes=1) for x in var]
    merged = [ga_ref[rs, :] * ((x * r * gng_ref[...] + gnb_ref[...] + bo_ref[rs, :]) * g_ref[rs, :]) + bm_ref[rs, :]
              for rs, x, r in zip(rows, d, rstd)]
    h = [x_ref[rs, :] + _dg(m.astype(BF16), wo_ref[...], NN) for rs, m in zip(rows, merged)]
    hn = [_rmsnorm(x, n2_ref[...]) for x in h]
    hb = [x.astype(BF16) for x in hn]
    sg = [_dg(x, wsg_ref[...], NN) for x in hb]
    su = [_dg(x, wsu_ref[...], NN) for x in hb]
    logits = [_mm3(wrt_ref[...], x, NT) for x in hn]
    shared = _map(lambda a, b: _dg((a * _sigmoid(a) * b).astype(BF16), wsd_ref[...], NN), sg, su)
    for rs, x, sh, xb, lg in zip(rows, h, shared, hb, logits):
        base_ref[rs, :] = x + sh
        hn_ref[rs, :] = xb
        scores = _sigmoid(lg)
        gate_t = _route(scores, scores + eb_ref[...])
        gate_pad = jnp.concatenate([gate_t, jnp.zeros((LANES - N_EXPERTS, gate_t.shape[1]), F32)], axis=0)
        gate_ref[rs, :] = gate_pad.T


def _post(o, g, bonus, ga, bm, x, p, tm):
    N, D = x.shape
    tm = min(tm, N)
    params = (p['gn_g'], p['gn_b'], p['w_out'], p['norm2_g'], p['ws_gate'], p['ws_up'], p['ws_down'],
              p['w_router_t'], p['e_bias'])
    row = pl.BlockSpec((tm, D), lambda i: (i, 0))
    return pl.pallas_call(
        _post_kernel,
        grid=(N // tm,),
        in_specs=[row] * 6 + [_full(a.shape) for a in params],
        out_specs=[row, row, pl.BlockSpec((tm, LANES), lambda i: (i, 0))],
        out_shape=[jax.ShapeDtypeStruct((N, D), F32), jax.ShapeDtypeStruct((N, D), BF16),
                   jax.ShapeDtypeStruct((N, LANES), F32)],
        compiler_params=_cparams(("parallel",)),
        name="post",
    )(o, g, bonus, ga, bm, x, *params)


def _moe_kernel(x_ref, gate_ref, base_ref, wg_ref, wu_ref, wd_ref, nf_ref, y_ref):
    j = pl.program_id(1)
    nj = pl.num_programs(1)
    eps = wg_ref.shape[0]

    @pl.when(j == 0)
    def _():
        y_ref[...] = base_ref[...]

    gate = gate_ref[...]
    lane = lax.broadcasted_iota(jnp.int32, gate.shape, 1)
    x = x_ref[...]
    cols = [jnp.sum(jnp.where(lane == j * eps + q, gate, 0.0), axis=1, keepdims=True) for q in range(eps)]
    hg = [_dg(x, wg_ref[q], NN) for q in range(eps)]
    hu = [_dg(x, wu_ref[q], NN) for q in range(eps)]
    hh = [(hg[q] * _sigmoid(hg[q]) * hu[q] * cols[q]).astype(BF16) for q in range(eps)]
    y_ref[...] += _dg(jnp.concatenate(hh, axis=1), wd_ref[...].reshape(eps * D_EXPERT, D_MODEL), NN)

    @pl.when(j == nj - 1)
    def _():
        y_ref[...] = _rmsnorm(y_ref[...], nf_ref[...])


def _moe(hn, gate, base, p, tm, eps):
    N, D = base.shape
    tm = min(tm, N)
    row = lambda w: pl.BlockSpec((tm, w), lambda i, e: (i, 0))
    return pl.pallas_call(
        _moe_kernel,
        grid=(N // tm, N_EXPERTS // eps),
        in_specs=[row(D), row(LANES), row(D),
                  pl.BlockSpec((eps, D, D_EXPERT), lambda i, e: (e, 0, 0)),
                  pl.BlockSpec((eps, D, D_EXPERT), lambda i, e: (e, 0, 0)),
                  pl.BlockSpec((eps, D_EXPERT, D), lambda i, e: (e, 0, 0)),
                  pl.BlockSpec((1, D), lambda i, e: (0, 0))],
        out_specs=row(D),
        out_shape=jax.ShapeDtypeStruct((N, D), F32),
        compiler_params=_cparams(("parallel", "arbitrary")),
        name="moe",
    )(hn, gate, base, p['w_gate'], p['w_up'], p['w_down'], p['normf_g'])


def kernel(x_prompt, x_sample, state_wkv, state_shift, state_conv, norm1_g, w_in, mu_shift, w0, w_decay_up, a0, a_up, g_up, k_k, k_a, r_k, gn_g, gn_b, conv_w, conv_b, cln_g, cln_b, w_out, norm2_g, w_router, e_bias, w_gate, w_up, w_down, ws_gate, ws_up, ws_down, normf_g):
    depth = w_in.shape[0]
    assert depth == 1
    B, T, D = x_prompt.shape
    NS = x_sample.shape[0]
    assert x_sample.shape[1] == 1 and D == D_MODEL and T % CHUNK == 0
    row = lambda a: a[0].reshape(1, -1)
    p = {
        'norm1_g': row(norm1_g), 'mu_shift': row(mu_shift), 'w0': row(w0), 'a0': row(a0),
        'k_k': row(k_k), 'k_a': row(k_a), 'r_k': row(r_k), 'gn_g': row(gn_g), 'gn_b': row(gn_b),
        'conv_b': row(conv_b), 'cln_g': row(cln_g), 'cln_b': row(cln_b), 'norm2_g': row(norm2_g),
        'normf_g': normf_g.reshape(1, -1),
        'w_rwkv': w_in[0, :, :C_RWKV].astype(BF16), 'w_rest': w_in[0, :, C_RWKV:].astype(BF16),
        'w_decay_up': w_decay_up[0], 'a_up': a_up[0], 'g_up': g_up[0], 'conv_w': conv_w[0],
        'w_out': w_out[0].astype(BF16),
        'ws_gate': ws_gate[0].astype(BF16), 'ws_up': ws_up[0].astype(BF16), 'ws_down': ws_down[0].astype(BF16),
        'w_router_t': w_router[0].T, 'e_bias': e_bias[0].reshape(-1, 1),
        'w_gate': w_gate[0].astype(BF16), 'w_up': w_up[0].astype(BF16), 'w_down': w_down[0].astype(BF16),
    }

    zp0 = jnp.zeros((B, 1, C_RWKV), F32)
    r, lw, k, v, kk, b, g, bonus, shift_p, ga, bm, conv_p = _mix_seq(
        x_prompt, zp0, jnp.zeros((B, CONV_WIDTH - 1, D), F32), p, tm=256)
    o, wkv_p = _wkv(r, lw, k, v, kk, b, rows=4 * CHUNK)
    flat = lambda a: a.reshape(B * T, D)
    base, hn, gate = _post(flat(o), flat(g), flat(bonus), flat(ga), flat(bm), flat(x_prompt), p, tm=512)
    y_prompt = _moe(hn, gate, base, p, tm=1024, eps=8).reshape(B, T, D)

    xs = x_sample.reshape(NS, D)
    r, lw, k, v, kk, b, g, bonus, shift_s = _rwkv_prep_batch(xs, state_shift[0], p)
    ga, bm, conv_s_t = _conv_branch_batch(xs, jnp.swapaxes(state_conv[0], 0, 1), p)
    o_t, wkv_s_t = _wkv_step(jnp.transpose(state_wkv[0], (1, 2, 3, 0)),
                             jnp.transpose(jnp.stack([r, lw, k, v, kk, b]), (0, 2, 1)))
    o = o_t.T
    wkv_s = jnp.transpose(wkv_s_t, (3, 0, 1, 2))
    base, hn, gate = _post(o, g, bonus, ga, bm, xs, p, tm=128)
    y_sample = _moe(hn, gate, base, p, tm=128, eps=8).reshape(NS, 1, D)

    return (y_prompt, y_sample, wkv_p[None], shift_p.reshape(1, B, D), conv_p[None],
            wkv_s[None], shift_s[None], jnp.swapaxes(conv_s_t, 0, 1)[None])
```

```python
import functools

import jax
import jax.numpy as jnp
from jax import lax
from jax.experimental import pallas as pl
from jax.experimental.pallas import tpu as pltpu

F32 = jnp.float32
BF16 = jnp.bfloat16

D_MODEL = 1024
HEAD_DIM = 64
N_HEADS = D_MODEL // HEAD_DIM
D_DECAY_LORA = 64
D_AAA_LORA = 64
D_GATE_LORA = 128
GN_EPS = 64e-5
CONV_WIDTH = 31
LN_EPS = 1e-5
N_EXPERTS = 64
N_GROUPS = 8
TOPK_GROUPS = 4
TOP_K = 8
D_EXPERT = 256
ROUTED_SCALE = 2.5
RMS_EPS = 1e-6
DECAY_SCALE = 0.6065306597126334

O_K = D_MODEL
O_V = 2 * D_MODEL
O_W = 3 * D_MODEL
O_A = O_W + D_DECAY_LORA
O_G = O_A + D_AAA_LORA
C_RWKV = O_G + D_GATE_LORA
C_REST = 4 * D_MODEL

LANES = 128
SUBLANES = 8
CHUNK = 64
HEADS_PER_GROUP = 2
GROUP = HEADS_PER_GROUP * HEAD_DIM
VMEM_BYTES = 64 * 1024 * 1024
VMEM_LIMIT = VMEM_BYTES - 4 * 1024 * 1024

TM_MIX = 256
ROWS_WKV = 4 * CHUNK
TM_POST = 512
TM_MOE = 1024
EXPERTS_PER_STEP = 8

NN = ((1,), (0,))
NT = ((1,), (1,))
TN = ((0,), (0,))


def _dg(a, b, dims):
    return lax.dot_general(a, b, (dims, ((), ())), preferred_element_type=F32)


def _split2(x):
    hi = x.astype(BF16)
    lo = (x - hi.astype(F32)).astype(BF16)
    return hi, lo


def _split3(x):
    hi = x.astype(BF16)
    r1 = x - hi.astype(F32)
    mid = r1.astype(BF16)
    lo = (r1 - mid.astype(F32)).astype(BF16)
    return hi, mid, lo


def _mm1(a, b, dims=NN):
    return _dg(a.astype(BF16), b.astype(BF16), dims)


def _mm3(a, b, dims=NN):
    ah, al = _split2(a)
    bh, bl = _split2(b)
    m = a.shape[0]
    if dims[0] == (1,) and m % SUBLANES == 0:
        top = _dg(jnp.concatenate([ah, al], axis=0), bh, dims)
        return top[0:m] + (_dg(ah, bl, dims) + top[m:2 * m])
    return _dg(ah, bh, dims) + (_dg(ah, bl, dims) + _dg(al, bh, dims))


def _mm_exact_rhs(a, b_bf16, dims=NN, passes=3):
    if passes == 1:
        return _dg(a.astype(BF16), b_bf16, dims)
    if passes == 2:
        h, l = _split2(a)
        return _dg(h, b_bf16, dims) + _dg(l, b_bf16, dims)
    h, m, l = _split3(a)
    return _dg(h, b_bf16, dims) + (_dg(m, b_bf16, dims) + _dg(l, b_bf16, dims))


def _rmsnorm(x, g):
    return x * lax.rsqrt(jnp.mean(x * x, axis=-1, keepdims=True) + RMS_EPS) * g


def _sigmoid(x):
    return 1.0 / (1.0 + jnp.exp(-x))


def _seg_mats():
    row = lax.broadcasted_iota(jnp.int32, (D_MODEL, LANES), 0) // HEAD_DIM
    col = lax.broadcasted_iota(jnp.int32, (D_MODEL, LANES), 1)
    seg = (row == col).astype(BF16)
    rowt = lax.broadcasted_iota(jnp.int32, (LANES, D_MODEL), 0)
    colt = lax.broadcasted_iota(jnp.int32, (LANES, D_MODEL), 1) // HEAD_DIM
    exp = (rowt == colt).astype(BF16)
    return seg, exp


def _cparams(sem):
    return pltpu.CompilerParams(dimension_semantics=sem, vmem_limit_bytes=VMEM_LIMIT)


def _full(shape):
    n = len(shape)
    return pl.BlockSpec(shape, lambda *_: (0,) * n)


def _prep_math(zr, zp, mu, w0, wdu, a0, aup, gup, kk_w, ka_w, rk_w, seg, exp):
    zm = zr + (zp - zr) * mu
    r = zm[:, 0:O_K]
    k = zm[:, O_K:O_V]
    v = zm[:, O_V:O_W]
    xw = jnp.tanh(zm[:, O_W:O_A])
    xa = zm[:, O_A:O_G]
    xg = _sigmoid(zm[:, O_G:C_RWKV])
    lw = -DECAY_SCALE * _sigmoid(w0 + _mm1(xw, wdu))
    a = _sigmoid(a0 + _mm1(xa, aup))
    g = _mm1(xg, gup)
    kkr = k * kk_w
    ss = _mm_exact_rhs(kkr * kkr, seg, passes=1)
    inv = 1.0 / jnp.maximum(jnp.sqrt(ss), 1e-12)
    kk = kkr * _mm_exact_rhs(inv, exp, passes=2)
    kf = k * (1.0 + (a - 1.0) * ka_w)
    b = kk * a
    rk = _mm_exact_rhs(r * kf * rk_w, seg, passes=1)
    bonus = _mm_exact_rhs(rk, exp, passes=1) * v
    return r, lw, kf, v, kk, b, g, bonus


def _prep_seq_body(zr, vec_refs, carry_ref):
    tm = zr.shape[0]
    rolled = pltpu.roll(zr, 1, 0)
    first = lax.broadcasted_iota(jnp.int32, (SUBLANES, zr.shape[1]), 0) == 0
    zp = jnp.concatenate([jnp.where(first, carry_ref[...], rolled[0:SUBLANES, :]), rolled[SUBLANES:, :]], axis=0)
    carry_ref[...] = zr[tm - 1:tm, :]
    seg, exp = _seg_mats()
    return _prep_math(zr, zp, *[ref[...] for ref in vec_refs], seg, exp)


def _prep_batch_kernel(x_ref, xp_ref, g1_ref, w_ref, mu_ref, w0_ref, wdu_ref, a0_ref, aup_ref, gup_ref,
                       kkw_ref, kaw_ref, rkw_ref,
                       r_ref, lw_ref, k_ref, v_ref, kk_ref, b_ref, g_ref, bo_ref, xn_ref):
    xn = _rmsnorm(x_ref[...], g1_ref[...])
    w = w_ref[...]
    zr = _dg(xn.astype(BF16), w, NN)
    zp = _dg(xp_ref[...].astype(BF16), w, NN)
    seg, exp = _seg_mats()
    outs = _prep_math(zr, zp, mu_ref[...], w0_ref[...], wdu_ref[...], a0_ref[...], aup_ref[...], gup_ref[...],
                      kkw_ref[...], kaw_ref[...], rkw_ref[...], seg, exp)
    for o_ref, val in zip((r_ref, lw_ref, k_ref, v_ref, kk_ref, b_ref, g_ref, bo_ref), outs):
        o_ref[...] = val
    xn_ref[...] = xn


def _prep_params(p):
    return (p['norm1_g'], p['w_rwkv'], p['mu_shift'], p['w0'], p['w_decay_up'], p['a0'], p['a_up'], p['g_up'],
            p['k_k'], p['k_a'], p['r_k'])


def _rwkv_prep_batch(x, xprev, p):
    N, D = x.shape
    params = _prep_params(p)
    out_shape = [jax.ShapeDtypeStruct((N, D), F32)] * 9
    return pl.pallas_call(
        _prep_batch_kernel,
        grid=(1,),
        in_specs=[_full(x.shape), _full(xprev.shape)] + [_full(a.shape) for a in params],
        out_specs=[_full((N, D))] * 9,
        out_shape=out_shape,
        compiler_params=_cparams(("arbitrary",)),
        name="rwkv_prep_batch",
    )(x, xprev, *params)


HIST = 32


def _conv_tail(c, cb, lg, lb, mix_a, mix_b):
    c = c + cb
    mean = jnp.mean(c, axis=-1, keepdims=True)
    d = c - mean
    var = jnp.mean(d * d, axis=-1, keepdims=True)
    y = d * lax.rsqrt(var + LN_EPS) * lg + lb
    out_b = y * _sigmoid(y)
    return _sigmoid(mix_a), _sigmoid(mix_b) * out_b


def _glu_mix(xn, w_ref):
    xb = xn.astype(BF16)
    glu_a = _dg(xb, w_ref[:, 0:D_MODEL], NN)
    glu_b = _dg(xb, w_ref[:, D_MODEL:2 * D_MODEL], NN)
    mix_a = _dg(xb, w_ref[:, 2 * D_MODEL:3 * D_MODEL], NN)
    mix_b = _dg(xb, w_ref[:, 3 * D_MODEL:4 * D_MODEL], NN)
    return glu_a * _sigmoid(glu_b), mix_a, mix_b


def _conv_taps(u, cw_ref, ubuf_ref):
    tm = u.shape[0]
    npast = CONV_WIDTH - 1
    ubuf_ref[pl.ds(HIST, tm), :] = u
    c = cw_ref[pl.ds(npast, 1), :] * ubuf_ref[pl.ds(HIST, tm), :]
    for s in range(SUBLANES):
        offs = [o for o in range(HIST - npast, HIST) if o % SUBLANES == s]
        grp = None
        for o in offs:
            term = cw_ref[pl.ds(o - (HIST - npast), 1), :] * ubuf_ref[pl.ds(o - s, tm + SUBLANES), :]
            grp = term if grp is None else grp + term
        c = c + grp[s:s + tm, :]
    return c


def _mix_seq_kernel(x_ref, zp0_ref, sc_ref, g1_ref, wr_ref, mu_ref, w0_ref, wdu_ref, a0_ref, aup_ref, gup_ref,
                    kkw_ref, kaw_ref, rkw_ref, wc_ref, cw_ref, cb_ref, lg_ref, lb_ref,
                    r_ref, lw_ref, k_ref, v_ref, kk_ref, b_ref, g_ref, bo_ref, xl_ref, ga_ref, bm_ref, so_ref,
                    carry_ref, ubuf_ref):
    t = pl.program_id(1)
    nt = pl.num_programs(1)
    tm = x_ref.shape[1]
    npast = CONV_WIDTH - 1

    @pl.when(t == 0)
    def _():
        carry_ref[...] = zp0_ref[0]
        ubuf_ref[pl.ds(0, HIST - npast), :] = jnp.zeros((HIST - npast, D_MODEL), F32)
        ubuf_ref[pl.ds(HIST - npast, npast), :] = sc_ref[0]

    @pl.when(t > 0)
    def _():
        ubuf_ref[pl.ds(0, HIST), :] = ubuf_ref[pl.ds(tm, HIST), :]

    xn = _rmsnorm(x_ref[0], g1_ref[...])
    xb = xn.astype(BF16)
    glu_a = _dg(xb, wc_ref[:, 0:D_MODEL], NN)
    glu_b = _dg(xb, wc_ref[:, D_MODEL:2 * D_MODEL], NN)
    c = _conv_taps(glu_a * _sigmoid(glu_b), cw_ref, ubuf_ref)
    zr = _dg(xb, wr_ref[...], NN)
    mix_a = _dg(xb, wc_ref[:, 2 * D_MODEL:3 * D_MODEL], NN)
    mix_b = _dg(xb, wc_ref[:, 3 * D_MODEL:4 * D_MODEL], NN)
    ga, bm = _conv_tail(c, cb_ref[...], lg_ref[...], lb_ref[...], mix_a, mix_b)
    outs = _prep_seq_body(zr, (mu_ref, w0_ref, wdu_ref, a0_ref, aup_ref, gup_ref, kkw_ref, kaw_ref, rkw_ref), carry_ref)
    for o_ref, val in zip((r_ref, lw_ref, k_ref, v_ref, kk_ref, b_ref, g_ref, bo_ref), outs):
        o_ref[0] = val
    xl_ref[0] = xn[tm - 1:tm, :]
    ga_ref[0] = ga
    bm_ref[0] = bm

    @pl.when(t == nt - 1)
    def _():
        so_ref[0] = ubuf_ref[pl.ds(tm + HIST - npast, npast), :]


def _conv_batch_kernel(x_ref, sc_ref, g1_ref, w_ref, cw_ref, cb_ref, lg_ref, lb_ref,
                       ga_ref, bm_ref, so_ref):
    npast = CONV_WIDTH - 1
    xn = _rmsnorm(x_ref[...], g1_ref[...])
    u, mix_a, mix_b = _glu_mix(xn, w_ref)
    c = cw_ref[pl.ds(npast, 1), :] * u
    for j in range(npast):
        c = c + cw_ref[pl.ds(j, 1), :] * sc_ref[j]
    ga, bm = _conv_tail(c, cb_ref[...], lg_ref[...], lb_ref[...], mix_a, mix_b)
    ga_ref[...] = ga
    bm_ref[...] = bm
    for j in range(npast - 1):
        so_ref[j] = sc_ref[j + 1]
    so_ref[npast - 1] = u


def _conv_params(p):
    return (p['norm1_g'], p['w_rest'], p['conv_w'], p['conv_b'], p['cln_g'], p['cln_b'])


def _resident(shape):
    n = len(shape)
    return pl.BlockSpec(shape, lambda *_: (0,) * n, pipeline_mode=pl.Buffered(1))


def _mix_seq(x, zp0, s_conv, p, tm):
    B, T, D = x.shape
    tm = min(tm, T)
    params = _prep_params(p) + _conv_params(p)[1:]
    npast = CONV_WIDTH - 1
    seq = pl.BlockSpec((1, tm, D), lambda b, t: (b, t, 0))
    per_b = lambda n: pl.BlockSpec((1, n, D), lambda b, t: (b, 0, 0))
    return pl.pallas_call(
        _mix_seq_kernel,
        grid=(B, T // tm),
        in_specs=[seq, pl.BlockSpec((1, 1, C_RWKV), lambda b, t: (b, 0, 0)), per_b(npast)]
                 + [_resident(a.shape) for a in params],
        out_specs=[seq] * 8 + [per_b(1), seq, seq, per_b(npast)],
        out_shape=[jax.ShapeDtypeStruct((B, T, D), F32)] * 8 + [jax.ShapeDtypeStruct((B, 1, D), F32)]
                  + [jax.ShapeDtypeStruct((B, T, D), F32)] * 2 + [jax.ShapeDtypeStruct((B, npast, D), F32)],
        scratch_shapes=[pltpu.VMEM((1, C_RWKV), F32), pltpu.VMEM((tm + HIST, D), F32)],
        compiler_params=_cparams(("parallel", "arbitrary")),
        name="mix_seq",
    )(x, zp0, s_conv, *params)


def _conv_branch_batch(x, s_conv_t, p):
    N, D = x.shape
    params = _conv_params(p)
    return pl.pallas_call(
        _conv_batch_kernel,
        grid=(1,),
        in_specs=[_full(x.shape), _full(s_conv_t.shape)] + [_full(a.shape) for a in params],
        out_specs=[_full((N, D)), _full((N, D)), _full(s_conv_t.shape)],
        out_shape=[jax.ShapeDtypeStruct((N, D), F32)] * 2 + [jax.ShapeDtypeStruct(s_conv_t.shape, F32)],
        compiler_params=_cparams(("arbitrary",)),
        name="conv_branch_batch",
    )(x, s_conv_t, *params)


def _head_ids(shape):
    return lax.broadcasted_iota(jnp.int32, shape, 1) // HEAD_DIM


def _bd(y, hid):
    zero = jnp.zeros_like(y)
    return jnp.concatenate([jnp.where(hid == h, y, zero) for h in range(HEADS_PER_GROUP)], axis=0)


def _diag_blocks(a, hid):
    out = a[(HEADS_PER_GROUP - 1) * HEAD_DIM:GROUP, :]
    for h in reversed(range(HEADS_PER_GROUP - 1)):
        out = jnp.where(hid == h, a[h * HEAD_DIM:(h + 1) * HEAD_DIM, :], out)
    return out


def _bdmm(x, y, hid, mm):
    return mm(x, _bd(y, hid), NN)


def _bdmm_nt(x, y, hid, mm):
    return mm(x, _bd(y, hid), NT)


def _bdmm_tn(x, y, hid, mm):
    return _diag_blocks(mm(x, y, TN), hid)


def _map(f, *lists):
    return [f(*xs) for xs in zip(*lists)]


def _chunk_groups(r, cum, lw, k, v, kk, b, hooks=()):
    L = CHUNK
    hooks = list(hooks)
    n_hooks, n_points, point = len(hooks), 8, [0]

    def run_hook():
        i = point[0]
        point[0] += 1
        if (i * n_hooks) // n_points != ((i + 1) * n_hooks) // n_points:
            hooks.pop(0)()

    hid = _head_ids((L, GROUP))
    trow = lax.broadcasted_iota(jnp.int32, (L, GROUP), 0)
    icol = lax.broadcasted_iota(jnp.int32, (L, GROUP), 1) % HEAD_DIM
    strict = icol < trow
    incl = icol <= trow
    eye = (icol == trow).astype(F32)
    zero = jnp.zeros((L, GROUP), F32)

    cl = [c[L - 1:L, :] for c in cum]
    alpha = _map(lambda x, c, w: x * jnp.exp(c - w), kk, cum, lw)
    rho = _map(lambda x, c: x * jnp.exp(c), r, cum)
    einv = [jnp.exp(-c) for c in cum]
    kappa = _map(lambda x, e: x * e, k, einv)
    beta = _map(lambda x, e: x * e, b, einv)
    etail = _map(lambda c1, c: jnp.exp(c1 - c), cl, cum)
    kappa2 = _map(lambda x, e: x * e, k, etail)
    beta2 = _map(lambda x, e: x * e, b, etail)
    dl = [jnp.exp(c1) for c1 in cl]

    mm = _mm1

    def rows2(x0, x1, y, dims):
        out = mm(jnp.concatenate([x0, x1], axis=0), _bd(y, hid), dims)
        return out[0:L, :], out[L:2 * L, :]

    def cols2(x, y0, y1):
        out = mm(x, jnp.concatenate([_bd(y0, hid), _bd(y1, hid)], axis=1), NN)
        return out[:, 0:GROUP], out[:, GROUP:2 * GROUP]

    def scores(a, q, y0, y1):
        out = mm(jnp.concatenate([a, q], axis=0), jnp.concatenate([_bd(y0, hid), _bd(y1, hid)], axis=0), NT)
        return out[0:L, 0:GROUP], out[L:2 * L, 0:GROUP], out[0:L, GROUP:2 * GROUP], out[L:2 * L, GROUP:2 * GROUP]

    sc = _map(scores, alpha, rho, kappa, beta)
    m_k = [jnp.where(strict, x[0], zero) for x in sc]
    n_k = [jnp.where(incl, x[1], zero) for x in sc]
    m_b = [jnp.where(strict, x[2], zero) for x in sc]
    n_b = [jnp.where(incl, x[3], zero) for x in sc]
    run_hook()

    nn = [-m for m in m_b]
    tinv = [eye + q for q in nn]
    pw = _map(lambda q: _bdmm(q, q, hid, mm), nn)
    mvn = _map(lambda a, q, y: rows2(a, q, y, NN), m_k, n_k, v)
    for it in range(4):
        res = _map(lambda q, t: rows2(q, t, q, NN), pw, tinv)
        tinv = _map(lambda t, x: t + x[1], tinv, res)
        pw = [x[0] for x in res]
        run_hook()
    tinv = _map(lambda t, q: t + _bdmm(t, q, hid, mm), tinv, pw)
    run_hook()

    mv = [x[0] for x in mvn]
    nkv = [x[1] for x in mvn]
    aw = _map(cols2, tinv, alpha, mv)
    alpha2 = [x[0] for x in aw]
    w = [x[1] for x in aw]
    run_hook()
    nb = _map(cols2, n_b, alpha2, w)
    rho2 = _map(lambda x, y: x - y[0], rho, nb)
    o2 = _map(lambda x, y: x - y[1], nkv, nb)
    run_hook()

    def tn2(x0, x1, y):
        a = mm(jnp.concatenate([x0, x1], axis=1), y, TN)
        return _diag_blocks(a[0:GROUP, :], hid), _diag_blocks(a[GROUP:2 * GROUP, :], hid)

    ab = _map(tn2, alpha2, w, beta2)
    g = _map(lambda d, x: eye * d - x[0], dl, ab)
    h = _map(lambda x, k2, y: _bdmm_tn(x, k2, hid, mm) - y[1], v, kappa2, ab)
    while hooks:
        hooks.pop(0)()
    return rho2, o2, g, h


def _wkv_kernel(r_ref, lw_ref, k_ref, v_ref, kk_ref, b_ref, o_ref, s_out_ref, rho_s, o2_s, g_s, h_s, s_ref):
    j = pl.program_id(1)
    n = pl.num_programs(1)
    L = CHUNK
    rows, width = r_ref.shape[1], r_ref.shape[2]
    nsub = rows // L
    hid = _head_ids((L, GROUP))
    sls = [slice(p * GROUP, (p + 1) * GROUP) for p in range(width // GROUP)]

    @pl.when(j == 0)
    def _():
        for ref in (rho_s, o2_s, g_s, h_s, s_ref):
            ref[...] = jnp.zeros_like(ref)

    state = [s_ref[:, sl] for sl in sls]

    def recur(c, row0):
        rs = pl.ds(c * L, L)
        o = [_bdmm_nt(rho_s[rs, sl], x, hid, _mm1) + o2_s[rs, sl] for sl, x in zip(sls, state)]
        state[:] = [_bdmm(x, g_s[rs, sl], hid, _mm3) + h_s[rs, sl] for sl, x in zip(sls, state)]
        dst = pl.ds(pl.multiple_of(row0 + c * L, L), L)
        for sl, ov in zip(sls, o):
            o_ref[0, dst, sl] = ov

    prev_row0 = jnp.maximum(j - 1, 0) * rows
    hooks = [functools.partial(recur, c, prev_row0) for c in range(nsub)]

    items = [(pl.ds(c * L, L), sl) for c in range(nsub) for sl in sls]
    lw = [lw_ref[0, rs, sl] for rs, sl in items]
    trow = lax.broadcasted_iota(jnp.int32, (L, GROUP), 0)
    cum = lw
    shift = 1
    while shift < L:
        cum = [x + jnp.where(trow >= shift, pltpu.roll(x, shift, 0), 0.0) for x in cum]
        shift *= 2
    pick = lambda ref: [ref[0, rs, sl] for rs, sl in items]
    outs = _chunk_groups(pick(r_ref), cum, lw, pick(k_ref), pick(v_ref), pick(kk_ref), pick(b_ref), hooks)
    for ref, vals in zip((rho_s, o2_s, g_s, h_s), outs):
        for (rs, sl), val in zip(items, vals):
            ref[rs, sl] = val
    for sl, sv in zip(sls, state):
        s_ref[:, sl] = sv

    @pl.when(j == n - 1)
    def _():
        state[:] = [s_ref[:, sl] for sl in sls]
        for c in range(nsub):
            recur(c, j * rows)
        for p, sv in enumerate(state):
            for h in range(HEADS_PER_GROUP):
                s_out_ref[0, HEADS_PER_GROUP * p + h] = sv[:, h * HEAD_DIM:(h + 1) * HEAD_DIM]


def _wkv(r, lw, k, v, kk, b, rows):
    B, T, D = r.shape
    blk = pl.BlockSpec((1, rows, D), lambda bi, c: (bi, c, 0))
    return pl.pallas_call(
        _wkv_kernel,
        grid=(B, T // rows),
        in_specs=[blk] * 6,
        out_specs=[pl.BlockSpec((1, T, D), lambda bi, c: (bi, 0, 0)),
                   pl.BlockSpec((1, N_HEADS, HEAD_DIM, HEAD_DIM), lambda bi, c: (bi, 0, 0, 0))],
        out_shape=[jax.ShapeDtypeStruct((B, T, D), F32),
                   jax.ShapeDtypeStruct((B, N_HEADS, HEAD_DIM, HEAD_DIM), F32)],
        scratch_shapes=[pltpu.VMEM((rows, D), F32)] * 4 + [pltpu.VMEM((HEAD_DIM, D), F32)],
        compiler_params=_cparams(("parallel", "arbitrary")),
        name="wkv",
    )(r, lw, k, v, kk, b)


def _wkv_step_kernel(s_ref, vec_ref, o_ref, so_ref):
    r, lw, k, v, kk, b = [vec_ref[i] for i in range(6)]
    d = jnp.exp(lw)
    for vi in range(HEAD_DIM):
        s = s_ref[0, vi]
        sa = -jnp.sum(s * kk, axis=0, keepdims=True)
        s_new = s * d + sa * b + v[vi:vi + 1, :] * k
        so_ref[0, vi] = s_new
        o_ref[pl.ds(vi, 1), :] = jnp.sum(s_new * r, axis=0, keepdims=True)


def _wkv_step(s_t, vecs_t):
    n = s_t.shape[-1]
    st = pl.BlockSpec((1, HEAD_DIM, HEAD_DIM, n), lambda h: (h, 0, 0, 0))
    return pl.pallas_call(
        _wkv_step_kernel,
        grid=(N_HEADS,),
        in_specs=[st, pl.BlockSpec((6, HEAD_DIM, n), lambda h: (0, h, 0))],
        out_specs=[pl.BlockSpec((HEAD_DIM, n), lambda h: (h, 0)), st],
        out_shape=[jax.ShapeDtypeStruct((D_MODEL, n), F32), jax.ShapeDtypeStruct(s_t.shape, F32)],
        compiler_params=_cparams(("parallel",)),
        name="wkv_step",
    )(s_t, vecs_t)


def _first_max(x, axis, n):
    m = jnp.max(x, axis=axis, keepdims=True)
    idx = lax.broadcasted_iota(jnp.int32, x.shape, axis)
    first = jnp.min(jnp.where(x == m, idx, n), axis=axis, keepdims=True)
    return m, idx == first


def _route(scores, biased):
    tm = scores.shape[1]
    per = N_EXPERTS // N_GROUPS
    neg = jnp.full((), -jnp.inf, F32)
    b3 = biased.reshape(N_GROUPS, per, tm)
    m1, hit = _first_max(b3, 1, per)
    m2 = jnp.max(jnp.where(hit, neg, b3), axis=1, keepdims=True)
    gs = (m1 + m2).reshape(N_GROUPS, tm)
    gsel = jnp.zeros((N_GROUPS, tm), jnp.bool_)
    for _ in range(TOPK_GROUPS):
        _, hit = _first_max(gs, 0, N_GROUPS)
        gsel = jnp.logical_or(gsel, hit)
        gs = jnp.where(hit, neg, gs)
    emask = jnp.broadcast_to(gsel.reshape(N_GROUPS, 1, tm), (N_GROUPS, per, tm)).reshape(N_EXPERTS, tm)
    cand = jnp.where(emask, biased, neg)
    esel = jnp.zeros((N_EXPERTS, tm), jnp.bool_)
    for _ in range(TOP_K):
        _, hit = _first_max(cand, 0, N_EXPERTS)
        esel = jnp.logical_or(esel, hit)
        cand = jnp.where(hit, neg, cand)
    wsel = jnp.where(esel, scores, 0.0)
    return wsel / jnp.sum(wsel, axis=0, keepdims=True) * ROUTED_SCALE


def _post_kernel(o_ref, g_ref, bo_ref, ga_ref, bm_ref, x_ref,
                 gng_ref, gnb_ref, wo_ref, n2_ref, wsg_ref, wsu_ref, wsd_ref, wrt_ref, eb_ref,
                 base_ref, hn_ref, gate_ref):
    seg, exp = _seg_mats()
    tm = o_ref.shape[0]
    nparts = 2 if tm % (2 * LANES) == 0 else 1
    rows = [pl.ds(i * (tm // nparts), tm // nparts) for i in range(nparts)]
    inv_n = 1.0 / HEAD_DIM
    o = [o_ref[rs, :] for rs in rows]
    mean = [_mm_exact_rhs(_mm_exact_rhs(x, seg, passes=1) * inv_n, exp, passes=2) for x in o]
    d = _map(lambda x, m: x - m, o, mean)
    var = [_mm_exact_rhs(x * x, seg, passes=1) * inv_n for x in d]
    rstd = [_mm_exact_rhs(lax.rsqrt(x + GN_EPS), exp, passes=1) for x in var]
    merged = [ga_ref[rs, :] * ((x * r * gng_ref[...] + gnb_ref[...] + bo_ref[rs, :]) * g_ref[rs, :]) + bm_ref[rs, :]
              for rs, x, r in zip(rows, d, rstd)]
    h = [x_ref[rs, :] + _dg(m.astype(BF16), wo_ref[...], NN) for rs, m in zip(rows, merged)]
    hn = [_rmsnorm(x, n2_ref[...]) for x in h]
    hb = [x.astype(BF16) for x in hn]
    sg = [_dg(x, wsg_ref[...], NN) for x in hb]
    su = [_dg(x, wsu_ref[...], NN) for x in hb]
    logits = [_mm3(wrt_ref[...], x, NT) for x in hn]
    shared = _map(lambda a, b: _dg((a * _sigmoid(a) * b).astype(BF16), wsd_ref[...], NN), sg, su)
    for rs, x, sh, xb, lg in zip(rows, h, shared, hb, logits):
        base_ref[rs, :] = x + sh
        hn_ref[rs, :] = xb
        scores = _sigmoid(lg)
        gate_t = _route(scores, scores + eb_ref[...])
        gate_pad = jnp.concatenate([gate_t, jnp.zeros((LANES - N_EXPERTS, gate_t.shape[1]), F32)], axis=0)
        gate_ref[rs, :] = gate_pad.T


def _post(o, g, bonus, ga, bm, x, p, tm):
    N, D = x.shape
    tm = min(tm, N)
    params = (p['gn_g'], p['gn_b'], p['w_out'], p['norm2_g'], p['ws_gate'], p['ws_up'], p['ws_down'],
              p['w_router_t'], p['e_bias'])
    row = pl.BlockSpec((tm, D), lambda i: (i, 0))
    return pl.pallas_call(
        _post_kernel,
        grid=(N // tm,),
        in_specs=[row] * 6 + [_full(a.shape) for a in params],
        out_specs=[row, row, pl.BlockSpec((tm, LANES), lambda i: (i, 0))],
        out_shape=[jax.ShapeDtypeStruct((N, D), F32), jax.ShapeDtypeStruct((N, D), BF16),
                   jax.ShapeDtypeStruct((N, LANES), F32)],
        compiler_params=_cparams(("parallel",)),
        name="post",
    )(o, g, bonus, ga, bm, x, *params)


def _moe_kernel(x_ref, gate_ref, base_ref, wg_ref, wu_ref, wd_ref, nf_ref, y_ref):
    j = pl.program_id(1)
    nj = pl.num_programs(1)
    eps = wg_ref.shape[0]

    @pl.when(j == 0)
    def _():
        y_ref[...] = base_ref[...]

    gate = gate_ref[...]
    lane = lax.broadcasted_iota(jnp.int32, gate.shape, 1)
    x = x_ref[...]
    cols = [jnp.sum(jnp.where(lane == j * eps + q, gate, 0.0), axis=1, keepdims=True) for q in range(eps)]
    hg = [_dg(x, wg_ref[q], NN) for q in range(eps)]
    hu = [_dg(x, wu_ref[q], NN) for q in range(eps)]
    hh = [(hg[q] * _sigmoid(hg[q]) * hu[q] * cols[q]).astype(BF16) for q in range(eps)]
    y_ref[...] += _dg(jnp.concatenate(hh, axis=1), wd_ref[...].reshape(eps * D_EXPERT, D_MODEL), NN)

    @pl.when(j == nj - 1)
    def _():
        y_ref[...] = _rmsnorm(y_ref[...], nf_ref[...])


def _moe(hn, gate, base, p, tm, eps):
    N, D = base.shape
    tm = min(tm, N)
    row = lambda w: pl.BlockSpec((tm, w), lambda i, e: (i, 0))
    return pl.pallas_call(
        _moe_kernel,
        grid=(N // tm, N_EXPERTS // eps),
        in_specs=[row(D), row(LANES), row(D),
                  pl.BlockSpec((eps, D, D_EXPERT), lambda i, e: (e, 0, 0)),
                  pl.BlockSpec((eps, D, D_EXPERT), lambda i, e: (e, 0, 0)),
                  pl.BlockSpec((eps, D_EXPERT, D), lambda i, e: (e, 0, 0)),
                  pl.BlockSpec((1, D), lambda i, e: (0, 0))],
        out_specs=row(D),
        out_shape=jax.ShapeDtypeStruct((N, D), F32),
        compiler_params=_cparams(("parallel", "arbitrary")),
        name="moe",
    )(hn, gate, base, p['w_gate'], p['w_up'], p['w_down'], p['normf_g'])


def kernel(x_prompt, x_sample, state_wkv, state_shift, state_conv, norm1_g, w_in, mu_shift, w0, w_decay_up, a0, a_up, g_up, k_k, k_a, r_k, gn_g, gn_b, conv_w, conv_b, cln_g, cln_b, w_out, norm2_g, w_router, e_bias, w_gate, w_up, w_down, ws_gate, ws_up, ws_down, normf_g):
    depth = w_in.shape[0]
    assert depth == 1
    B, T, D = x_prompt.shape
    NS = x_sample.shape[0]
    assert x_sample.shape[1] == 1 and D == D_MODEL and T % CHUNK == 0
    row = lambda a: a[0].reshape(1, -1)
    p = {
        'norm1_g': row(norm1_g), 'mu_shift': row(mu_shift), 'w0': row(w0), 'a0': row(a0),
        'k_k': row(k_k), 'k_a': row(k_a), 'r_k': row(r_k), 'gn_g': row(gn_g), 'gn_b': row(gn_b),
        'conv_b': row(conv_b), 'cln_g': row(cln_g), 'cln_b': row(cln_b), 'norm2_g': row(norm2_g),
        'normf_g': normf_g.reshape(1, -1),
        'w_rwkv': w_in[0, :, :C_RWKV].astype(BF16), 'w_rest': w_in[0, :, C_RWKV:].astype(BF16),
        'w_decay_up': w_decay_up[0], 'a_up': a_up[0], 'g_up': g_up[0], 'conv_w': conv_w[0],
        'w_out': w_out[0].astype(BF16),
        'ws_gate': ws_gate[0].astype(BF16), 'ws_up': ws_up[0].astype(BF16), 'ws_down': ws_down[0].astype(BF16),
        'w_router_t': w_router[0].T, 'e_bias': e_bias[0].reshape(-1, 1),
        'w_gate': w_gate[0].astype(BF16), 'w_up': w_up[0].astype(BF16), 'w_down': w_down[0].astype(BF16),
    }

    zp0 = jnp.zeros((B, 1, C_RWKV), F32)
    r, lw, k, v, kk, b, g, bonus, shift_p, ga, bm, conv_p = _mix_seq(
        x_prompt, zp0, jnp.zeros((B, CONV_WIDTH - 1, D), F32), p, tm=TM_MIX)
    o, wkv_p = _wkv(r, lw, k, v, kk, b, rows=min(ROWS_WKV, T))
    flat = lambda a: a.reshape(B * T, D)
    base, hn, gate = _post(flat(o), flat(g), flat(bonus), flat(ga), flat(bm), flat(x_prompt), p, tm=TM_POST)
    y_prompt = _moe(hn, gate, base, p, tm=TM_MOE, eps=EXPERTS_PER_STEP).reshape(B, T, D)

    xs = x_sample.reshape(NS, D)
    r, lw, k, v, kk, b, g, bonus, shift_s = _rwkv_prep_batch(xs, state_shift[0], p)
    ga, bm, conv_s_t = _conv_branch_batch(xs, jnp.swapaxes(state_conv[0], 0, 1), p)
    o_t, wkv_s_t = _wkv_step(jnp.transpose(state_wkv[0], (1, 2, 3, 0)),
                             jnp.transpose(jnp.stack([r, lw, k, v, kk, b]), (0, 2, 1)))
    o = o_t.T
    wkv_s = jnp.transpose(wkv_s_t, (3, 0, 1, 2))
    base, hn, gate = _post(o, g, bonus, ga, bm, xs, p, tm=TM_POST)
    y_sample = _moe(hn, gate, base, p, tm=TM_MOE, eps=EXPERTS_PER_STEP).reshape(NS, 1, D)

    return (y_prompt, y_sample, wkv_p[None], shift_p.reshape(1, B, D), conv_p[None],
            wkv_s[None], shift_s[None], jnp.swapaxes(conv_s_t, 0, 1)[None])
```

```python
import functools

import jax
import jax.numpy as jnp
from jax import lax
from jax.experimental import pallas as pl
from jax.experimental.pallas import tpu as pltpu

F32 = jnp.float32
BF16 = jnp.bfloat16

D_MODEL = 1024
HEAD_DIM = 64
N_HEADS = D_MODEL // HEAD_DIM
D_DECAY_LORA = 64
D_AAA_LORA = 64
D_GATE_LORA = 128
GN_EPS = 64e-5
CONV_WIDTH = 31
LN_EPS = 1e-5
N_EXPERTS = 64
N_GROUPS = 8
TOPK_GROUPS = 4
TOP_K = 8
D_EXPERT = 256
ROUTED_SCALE = 2.5
RMS_EPS = 1e-6
DECAY_SCALE = 0.6065306597126334

O_K = D_MODEL
O_V = 2 * D_MODEL
O_W = 3 * D_MODEL
O_A = O_W + D_DECAY_LORA
O_G = O_A + D_AAA_LORA
C_RWKV = O_G + D_GATE_LORA
COL_GLU_A, COL_GLU_B, COL_MIX_A, COL_MIX_B = (
    slice(C_RWKV + i * D_MODEL, C_RWKV + (i + 1) * D_MODEL) for i in range(4))

LANES = 128
SUBLANES = 8
CHUNK = 64
HEADS_PER_GROUP = 2
GROUP = HEADS_PER_GROUP * HEAD_DIM
VMEM_BYTES = 64 * 1024 * 1024
VMEM_LIMIT = VMEM_BYTES - 4 * 1024 * 1024

TM_MIX = 256
ROWS_WKV = 4 * CHUNK
TM_POST = 512
TM_MOE = 1024
EXPERTS_PER_STEP = 8

NN = ((1,), (0,))
NT = ((1,), (1,))
TN = ((0,), (0,))


def _dg(a, b, dims):
    return lax.dot_general(a, b, (dims, ((), ())), preferred_element_type=F32)


def _split2(x):
    hi = x.astype(BF16)
    lo = (x - hi.astype(F32)).astype(BF16)
    return hi, lo


def _split3(x):
    hi = x.astype(BF16)
    r1 = x - hi.astype(F32)
    mid = r1.astype(BF16)
    lo = (r1 - mid.astype(F32)).astype(BF16)
    return hi, mid, lo


def _mm1(a, b, dims=NN):
    return _dg(a.astype(BF16), b.astype(BF16), dims)


def _mm3(a, b, dims=NN):
    ah, al = _split2(a)
    bh, bl = _split2(b)
    m = a.shape[0]
    if dims[0] == (1,) and m % SUBLANES == 0:
        top = _dg(jnp.concatenate([ah, al], axis=0), bh, dims)
        return top[0:m] + (_dg(ah, bl, dims) + top[m:2 * m])
    return _dg(ah, bh, dims) + (_dg(ah, bl, dims) + _dg(al, bh, dims))


def _mm_exact_rhs(a, b_bf16, dims=NN, passes=3):
    if passes == 1:
        return _dg(a.astype(BF16), b_bf16, dims)
    if passes == 2:
        h, l = _split2(a)
        return _dg(h, b_bf16, dims) + _dg(l, b_bf16, dims)
    h, m, l = _split3(a)
    return _dg(h, b_bf16, dims) + (_dg(m, b_bf16, dims) + _dg(l, b_bf16, dims))


def _rmsnorm(x, g):
    return x * lax.rsqrt(jnp.mean(x * x, axis=-1, keepdims=True) + RMS_EPS) * g


def _sigmoid(x):
    return 1.0 / (1.0 + jnp.exp(-x))


def _seg_mats():
    row = lax.broadcasted_iota(jnp.int32, (D_MODEL, LANES), 0) // HEAD_DIM
    col = lax.broadcasted_iota(jnp.int32, (D_MODEL, LANES), 1)
    seg = (row == col).astype(BF16)
    rowt = lax.broadcasted_iota(jnp.int32, (LANES, D_MODEL), 0)
    colt = lax.broadcasted_iota(jnp.int32, (LANES, D_MODEL), 1) // HEAD_DIM
    exp = (rowt == colt).astype(BF16)
    return seg, exp


def _cparams(sem):
    return pltpu.CompilerParams(dimension_semantics=sem, vmem_limit_bytes=VMEM_LIMIT)


def _full(shape):
    n = len(shape)
    return pl.BlockSpec(shape, lambda *_: (0,) * n)


def _prep_math(zr, zp, mu, w0, wdu, a0, aup, gup, kk_w, ka_w, rk_w, seg, exp):
    zm = zr + (zp - zr) * mu
    r = zm[:, 0:O_K]
    k = zm[:, O_K:O_V]
    v = zm[:, O_V:O_W]
    xw = jnp.tanh(zm[:, O_W:O_A])
    xa = zm[:, O_A:O_G]
    xg = _sigmoid(zm[:, O_G:C_RWKV])
    lw = -DECAY_SCALE * _sigmoid(w0 + _mm1(xw, wdu))
    a = _sigmoid(a0 + _mm1(xa, aup))
    g = _mm1(xg, gup)
    kkr = k * kk_w
    ss = _mm_exact_rhs(kkr * kkr, seg, passes=1)
    inv = 1.0 / jnp.maximum(jnp.sqrt(ss), 1e-12)
    kk = kkr * _mm_exact_rhs(inv, exp, passes=2)
    kf = k * (1.0 + (a - 1.0) * ka_w)
    b = kk * a
    rk = _mm_exact_rhs(r * kf * rk_w, seg, passes=1)
    bonus = _mm_exact_rhs(rk, exp, passes=1) * v
    return r, lw, kf, v, kk, b, g, bonus


def _prep_seq_body(zr, vec_refs, carry_ref):
    tm = zr.shape[0]
    rolled = pltpu.roll(zr, 1, 0)
    first = lax.broadcasted_iota(jnp.int32, (SUBLANES, zr.shape[1]), 0) == 0
    zp = jnp.concatenate([jnp.where(first, carry_ref[...], rolled[0:SUBLANES, :]), rolled[SUBLANES:, :]], axis=0)
    carry_ref[...] = zr[tm - 1:tm, :]
    seg, exp = _seg_mats()
    return _prep_math(zr, zp, *[ref[...] for ref in vec_refs], seg, exp)


def _prep_batch_kernel(x_ref, xp_ref, g1_ref, w_ref, mu_ref, w0_ref, wdu_ref, a0_ref, aup_ref, gup_ref,
                       kkw_ref, kaw_ref, rkw_ref,
                       r_ref, lw_ref, k_ref, v_ref, kk_ref, b_ref, g_ref, bo_ref, xn_ref):
    xn = _rmsnorm(x_ref[...], g1_ref[...])
    w = w_ref[:, 0:C_RWKV]
    zr = _dg(xn.astype(BF16), w, NN)
    zp = _dg(xp_ref[...].astype(BF16), w, NN)
    seg, exp = _seg_mats()
    outs = _prep_math(zr, zp, mu_ref[...], w0_ref[...], wdu_ref[...], a0_ref[...], aup_ref[...], gup_ref[...],
                      kkw_ref[...], kaw_ref[...], rkw_ref[...], seg, exp)
    for o_ref, val in zip((r_ref, lw_ref, k_ref, v_ref, kk_ref, b_ref, g_ref, bo_ref), outs):
        o_ref[...] = val
    xn_ref[...] = xn


def _prep_params(p):
    return (p['norm1_g'], p['w_in'], p['mu_shift'], p['w0'], p['w_decay_up'], p['a0'], p['a_up'], p['g_up'],
            p['k_k'], p['k_a'], p['r_k'])


def _rwkv_prep_batch(x, xprev, p):
    N, D = x.shape
    params = _prep_params(p)
    out_shape = [jax.ShapeDtypeStruct((N, D), F32)] * 9
    return pl.pallas_call(
        _prep_batch_kernel,
        grid=(1,),
        in_specs=[_full(x.shape), _full(xprev.shape)] + [_full(a.shape) for a in params],
        out_specs=[_full((N, D))] * 9,
        out_shape=out_shape,
        compiler_params=_cparams(("arbitrary",)),
        name="rwkv_prep_batch",
    )(x, xprev, *params)


HIST = 32


def _conv_tail(c, cb, lg, lb, mix_a, mix_b):
    c = c + cb
    mean = jnp.mean(c, axis=-1, keepdims=True)
    d = c - mean
    var = jnp.mean(d * d, axis=-1, keepdims=True)
    y = d * lax.rsqrt(var + LN_EPS) * lg + lb
    out_b = y * _sigmoid(y)
    return _sigmoid(mix_a), _sigmoid(mix_b) * out_b


def _glu_mix(xn, w_ref):
    xb = xn.astype(BF16)
    glu_a = _dg(xb, w_ref[:, COL_GLU_A], NN)
    glu_b = _dg(xb, w_ref[:, COL_GLU_B], NN)
    mix_a = _dg(xb, w_ref[:, COL_MIX_A], NN)
    mix_b = _dg(xb, w_ref[:, COL_MIX_B], NN)
    return glu_a * _sigmoid(glu_b), mix_a, mix_b


def _conv_taps(u, cw_ref, ubuf_ref):
    tm = u.shape[0]
    npast = CONV_WIDTH - 1
    ubuf_ref[pl.ds(HIST, tm), :] = u
    c = cw_ref[pl.ds(npast, 1), :] * ubuf_ref[pl.ds(HIST, tm), :]
    for s in range(SUBLANES):
        offs = [o for o in range(HIST - npast, HIST) if o % SUBLANES == s]
        grp = None
        for o in offs:
            term = cw_ref[pl.ds(o - (HIST - npast), 1), :] * ubuf_ref[pl.ds(o - s, tm + SUBLANES), :]
            grp = term if grp is None else grp + term
        c = c + grp[s:s + tm, :]
    return c


def _mix_seq_kernel(x_ref, zp0_ref, sc_ref, g1_ref, w_ref, mu_ref, w0_ref, wdu_ref, a0_ref, aup_ref, gup_ref,
                    kkw_ref, kaw_ref, rkw_ref, cw_ref, cb_ref, lg_ref, lb_ref,
                    r_ref, lw_ref, k_ref, v_ref, kk_ref, b_ref, g_ref, bo_ref, xl_ref, ga_ref, bm_ref, so_ref,
                    carry_ref, ubuf_ref):
    t = pl.program_id(1)
    nt = pl.num_programs(1)
    tm = x_ref.shape[1]
    npast = CONV_WIDTH - 1

    @pl.when(t == 0)
    def _():
        carry_ref[...] = zp0_ref[0]
        ubuf_ref[pl.ds(0, HIST - npast), :] = jnp.zeros((HIST - npast, D_MODEL), F32)
        ubuf_ref[pl.ds(HIST - npast, npast), :] = sc_ref[0]

    @pl.when(t > 0)
    def _():
        ubuf_ref[pl.ds(0, HIST), :] = ubuf_ref[pl.ds(tm, HIST), :]

    xn = _rmsnorm(x_ref[0], g1_ref[...])
    xb = xn.astype(BF16)
    glu_a = _dg(xb, w_ref[:, COL_GLU_A], NN)
    glu_b = _dg(xb, w_ref[:, COL_GLU_B], NN)
    c = _conv_taps(glu_a * _sigmoid(glu_b), cw_ref, ubuf_ref)
    zr = _dg(xb, w_ref[:, 0:C_RWKV], NN)
    mix_a = _dg(xb, w_ref[:, COL_MIX_A], NN)
    mix_b = _dg(xb, w_ref[:, COL_MIX_B], NN)
    ga, bm = _conv_tail(c, cb_ref[...], lg_ref[...], lb_ref[...], mix_a, mix_b)
    outs = _prep_seq_body(zr, (mu_ref, w0_ref, wdu_ref, a0_ref, aup_ref, gup_ref, kkw_ref, kaw_ref, rkw_ref), carry_ref)
    for o_ref, val in zip((r_ref, lw_ref, k_ref, v_ref, kk_ref, b_ref, g_ref, bo_ref), outs):
        o_ref[0] = val
    xl_ref[0] = xn[tm - 1:tm, :]
    ga_ref[0] = ga
    bm_ref[0] = bm

    @pl.when(t == nt - 1)
    def _():
        so_ref[0] = ubuf_ref[pl.ds(tm + HIST - npast, npast), :]


def _conv_batch_kernel(x_ref, sc_ref, g1_ref, w_ref, cw_ref, cb_ref, lg_ref, lb_ref,
                       ga_ref, bm_ref, so_ref):
    npast = CONV_WIDTH - 1
    xn = _rmsnorm(x_ref[...], g1_ref[...])
    u, mix_a, mix_b = _glu_mix(xn, w_ref)
    c = cw_ref[pl.ds(npast, 1), :] * u
    for j in range(npast):
        c = c + cw_ref[pl.ds(j, 1), :] * sc_ref[j]
    ga, bm = _conv_tail(c, cb_ref[...], lg_ref[...], lb_ref[...], mix_a, mix_b)
    ga_ref[...] = ga
    bm_ref[...] = bm
    for j in range(npast - 1):
        so_ref[j] = sc_ref[j + 1]
    so_ref[npast - 1] = u


def _conv_params(p):
    return (p['norm1_g'], p['w_in'], p['conv_w'], p['conv_b'], p['cln_g'], p['cln_b'])


def _resident(shape):
    n = len(shape)
    return pl.BlockSpec(shape, lambda *_: (0,) * n, pipeline_mode=pl.Buffered(1))


def _mix_seq(x, zp0, s_conv, p, tm):
    B, T, D = x.shape
    tm = min(tm, T)
    params = _prep_params(p) + _conv_params(p)[2:]
    npast = CONV_WIDTH - 1
    seq = pl.BlockSpec((1, tm, D), lambda b, t: (b, t, 0))
    per_b = lambda n: pl.BlockSpec((1, n, D), lambda b, t: (b, 0, 0))
    return pl.pallas_call(
        _mix_seq_kernel,
        grid=(B, T // tm),
        in_specs=[seq, pl.BlockSpec((1, 1, C_RWKV), lambda b, t: (b, 0, 0)), per_b(npast)]
                 + [_resident(a.shape) for a in params],
        out_specs=[seq] * 8 + [per_b(1), seq, seq, per_b(npast)],
        out_shape=[jax.ShapeDtypeStruct((B, T, D), F32)] * 8 + [jax.ShapeDtypeStruct((B, 1, D), F32)]
                  + [jax.ShapeDtypeStruct((B, T, D), F32)] * 2 + [jax.ShapeDtypeStruct((B, npast, D), F32)],
        scratch_shapes=[pltpu.VMEM((1, C_RWKV), F32), pltpu.VMEM((tm + HIST, D), F32)],
        compiler_params=_cparams(("parallel", "arbitrary")),
        name="mix_seq",
    )(x, zp0, s_conv, *params)


def _conv_branch_batch(x, s_conv_t, p):
    N, D = x.shape
    params = _conv_params(p)
    return pl.pallas_call(
        _conv_batch_kernel,
        grid=(1,),
        in_specs=[_full(x.shape), _full(s_conv_t.shape)] + [_full(a.shape) for a in params],
        out_specs=[_full((N, D)), _full((N, D)), _full(s_conv_t.shape)],
        out_shape=[jax.ShapeDtypeStruct((N, D), F32)] * 2 + [jax.ShapeDtypeStruct(s_conv_t.shape, F32)],
        compiler_params=_cparams(("arbitrary",)),
        name="conv_branch_batch",
    )(x, s_conv_t, *params)


def _head_ids(shape):
    return lax.broadcasted_iota(jnp.int32, shape, 1) // HEAD_DIM


def _bd(y, hid):
    zero = jnp.zeros_like(y)
    return jnp.concatenate([jnp.where(hid == h, y, zero) for h in range(HEADS_PER_GROUP)], axis=0)


def _diag_blocks(a, hid):
    out = a[(HEADS_PER_GROUP - 1) * HEAD_DIM:GROUP, :]
    for h in reversed(range(HEADS_PER_GROUP - 1)):
        out = jnp.where(hid == h, a[h * HEAD_DIM:(h + 1) * HEAD_DIM, :], out)
    return out


def _bdmm(x, y, hid, mm):
    return mm(x, _bd(y, hid), NN)


def _bdmm_nt(x, y, hid, mm):
    return mm(x, _bd(y, hid), NT)


def _bdmm_tn(x, y, hid, mm):
    return _diag_blocks(mm(x, y, TN), hid)


def _map(f, *lists):
    return [f(*xs) for xs in zip(*lists)]


def _chunk_groups(r, cum, lw, k, v, kk, b, hooks=()):
    L = CHUNK
    hooks = list(hooks)
    n_hooks, n_points, point = len(hooks), 8, [0]

    def run_hook():
        i = point[0]
        point[0] += 1
        if (i * n_hooks) // n_points != ((i + 1) * n_hooks) // n_points:
            hooks.pop(0)()

    hid = _head_ids((L, GROUP))
    trow = lax.broadcasted_iota(jnp.int32, (L, GROUP), 0)
    icol = lax.broadcasted_iota(jnp.int32, (L, GROUP), 1) % HEAD_DIM
    strict = icol < trow
    incl = icol <= trow
    eye = (icol == trow).astype(F32)
    zero = jnp.zeros((L, GROUP), F32)

    cl = [c[L - 1:L, :] for c in cum]
    alpha = _map(lambda x, c, w: x * jnp.exp(c - w), kk, cum, lw)
    rho = _map(lambda x, c: x * jnp.exp(c), r, cum)
    einv = [jnp.exp(-c) for c in cum]
    kappa = _map(lambda x, e: x * e, k, einv)
    beta = _map(lambda x, e: x * e, b, einv)
    etail = _map(lambda c1, c: jnp.exp(c1 - c), cl, cum)
    kappa2 = _map(lambda x, e: x * e, k, etail)
    beta2 = _map(lambda x, e: x * e, b, etail)
    dl = [jnp.exp(c1) for c1 in cl]

    mm = _mm1

    def rows2(x0, x1, y, dims):
        out = mm(jnp.concatenate([x0, x1], axis=0), _bd(y, hid), dims)
        return out[0:L, :], out[L:2 * L, :]

    def cols2(x, y0, y1):
        out = mm(x, jnp.concatenate([_bd(y0, hid), _bd(y1, hid)], axis=1), NN)
        return out[:, 0:GROUP], out[:, GROUP:2 * GROUP]

    def scores(a, q, y0, y1):
        out = mm(jnp.concatenate([a, q], axis=0), jnp.concatenate([_bd(y0, hid), _bd(y1, hid)], axis=0), NT)
        return out[0:L, 0:GROUP], out[L:2 * L, 0:GROUP], out[0:L, GROUP:2 * GROUP], out[L:2 * L, GROUP:2 * GROUP]

    sc = _map(scores, alpha, rho, kappa, beta)
    m_k = [jnp.where(strict, x[0], zero) for x in sc]
    n_k = [jnp.where(incl, x[1], zero) for x in sc]
    m_b = [jnp.where(strict, x[2], zero) for x in sc]
    n_b = [jnp.where(incl, x[3], zero) for x in sc]
    run_hook()

    nn = [-m for m in m_b]
    tinv = [eye + q for q in nn]
    pw = _map(lambda q: _bdmm(q, q, hid, mm), nn)
    mvn = _map(lambda a, q, y: rows2(a, q, y, NN), m_k, n_k, v)
    for it in range(4):
        res = _map(lambda q, t: rows2(q, t, q, NN), pw, tinv)
        tinv = _map(lambda t, x: t + x[1], tinv, res)
        pw = [x[0] for x in res]
        run_hook()
    tinv = _map(lambda t, q: t + _bdmm(t, q, hid, mm), tinv, pw)
    run_hook()

    mv = [x[0] for x in mvn]
    nkv = [x[1] for x in mvn]
    aw = _map(cols2, tinv, alpha, mv)
    alpha2 = [x[0] for x in aw]
    w = [x[1] for x in aw]
    run_hook()
    nb = _map(cols2, n_b, alpha2, w)
    rho2 = _map(lambda x, y: x - y[0], rho, nb)
    o2 = _map(lambda x, y: x - y[1], nkv, nb)
    run_hook()

    def tn2(x0, x1, y):
        a = mm(jnp.concatenate([x0, x1], axis=1), y, TN)
        return _diag_blocks(a[0:GROUP, :], hid), _diag_blocks(a[GROUP:2 * GROUP, :], hid)

    ab = _map(tn2, alpha2, w, beta2)
    g = _map(lambda d, x: eye * d - x[0], dl, ab)
    h = _map(lambda x, k2, y: _bdmm_tn(x, k2, hid, mm) - y[1], v, kappa2, ab)
    while hooks:
        hooks.pop(0)()
    return rho2, o2, g, h


def _wkv_kernel(r_ref, lw_ref, k_ref, v_ref, kk_ref, b_ref, o_ref, s_out_ref, rho_s, o2_s, g_s, h_s, s_ref):
    j = pl.program_id(1)
    n = pl.num_programs(1)
    L = CHUNK
    rows, width = r_ref.shape[1], r_ref.shape[2]
    nsub = rows // L
    hid = _head_ids((L, GROUP))
    sls = [slice(p * GROUP, (p + 1) * GROUP) for p in range(width // GROUP)]

    @pl.when(j == 0)
    def _():
        for ref in (rho_s, o2_s, g_s, h_s, s_ref):
            ref[...] = jnp.zeros_like(ref)

    state = [s_ref[:, sl] for sl in sls]

    def recur(c, row0):
        rs = pl.ds(c * L, L)
        o = [_bdmm_nt(rho_s[rs, sl], x, hid, _mm1) + o2_s[rs, sl] for sl, x in zip(sls, state)]
        state[:] = [_bdmm(x, g_s[rs, sl], hid, _mm3) + h_s[rs, sl] for sl, x in zip(sls, state)]
        dst = pl.ds(pl.multiple_of(row0 + c * L, L), L)
        for sl, ov in zip(sls, o):
            o_ref[0, dst, sl] = ov

    prev_row0 = jnp.maximum(j - 1, 0) * rows
    hooks = [functools.partial(recur, c, prev_row0) for c in range(nsub)]

    items = [(pl.ds(c * L, L), sl) for c in range(nsub) for sl in sls]
    lw = [lw_ref[0, rs, sl] for rs, sl in items]
    trow = lax.broadcasted_iota(jnp.int32, (L, GROUP), 0)
    cum = lw
    shift = 1
    while shift < L:
        cum = [x + jnp.where(trow >= shift, pltpu.roll(x, shift, 0), 0.0) for x in cum]
        shift *= 2
    pick = lambda ref: [ref[0, rs, sl] for rs, sl in items]
    outs = _chunk_groups(pick(r_ref), cum, lw, pick(k_ref), pick(v_ref), pick(kk_ref), pick(b_ref), hooks)
    for ref, vals in zip((rho_s, o2_s, g_s, h_s), outs):
        for (rs, sl), val in zip(items, vals):
            ref[rs, sl] = val
    for sl, sv in zip(sls, state):
        s_ref[:, sl] = sv

    @pl.when(j == n - 1)
    def _():
        state[:] = [s_ref[:, sl] for sl in sls]
        for c in range(nsub):
            recur(c, j * rows)
        for p, sv in enumerate(state):
            for h in range(HEADS_PER_GROUP):
                s_out_ref[0, HEADS_PER_GROUP * p + h] = sv[:, h * HEAD_DIM:(h + 1) * HEAD_DIM]


def _wkv(r, lw, k, v, kk, b, rows):
    B, T, D = r.shape
    blk = pl.BlockSpec((1, rows, D), lambda bi, c: (bi, c, 0))
    return pl.pallas_call(
        _wkv_kernel,
        grid=(B, T // rows),
        in_specs=[blk] * 6,
        out_specs=[pl.BlockSpec((1, T, D), lambda bi, c: (bi, 0, 0)),
                   pl.BlockSpec((1, N_HEADS, HEAD_DIM, HEAD_DIM), lambda bi, c: (bi, 0, 0, 0))],
        out_shape=[jax.ShapeDtypeStruct((B, T, D), F32),
                   jax.ShapeDtypeStruct((B, N_HEADS, HEAD_DIM, HEAD_DIM), F32)],
        scratch_shapes=[pltpu.VMEM((rows, D), F32)] * 4 + [pltpu.VMEM((HEAD_DIM, D), F32)],
        compiler_params=_cparams(("parallel", "arbitrary")),
        name="wkv",
    )(r, lw, k, v, kk, b)


def _wkv_step_kernel(s_ref, vec_ref, o_ref, so_ref):
    r, lw, k, v, kk, b = [vec_ref[i] for i in range(6)]
    d = jnp.exp(lw)
    for vi in range(HEAD_DIM):
        s = s_ref[0, vi]
        sa = -jnp.sum(s * kk, axis=0, keepdims=True)
        s_new = s * d + sa * b + v[vi:vi + 1, :] * k
        so_ref[0, vi] = s_new
        o_ref[pl.ds(vi, 1), :] = jnp.sum(s_new * r, axis=0, keepdims=True)


def _wkv_step(s_t, vecs_t):
    n = s_t.shape[-1]
    st = pl.BlockSpec((1, HEAD_DIM, HEAD_DIM, n), lambda h: (h, 0, 0, 0))
    return pl.pallas_call(
        _wkv_step_kernel,
        grid=(N_HEADS,),
        in_specs=[st, pl.BlockSpec((6, HEAD_DIM, n), lambda h: (0, h, 0))],
        out_specs=[pl.BlockSpec((HEAD_DIM, n), lambda h: (h, 0)), st],
        out_shape=[jax.ShapeDtypeStruct((D_MODEL, n), F32), jax.ShapeDtypeStruct(s_t.shape, F32)],
        compiler_params=_cparams(("parallel",)),
        name="wkv_step",
    )(s_t, vecs_t)


def _first_max(x, axis, n):
    m = jnp.max(x, axis=axis, keepdims=True)
    idx = lax.broadcasted_iota(jnp.int32, x.shape, axis)
    first = jnp.min(jnp.where(x == m, idx, n), axis=axis, keepdims=True)
    return m, idx == first


def _route(scores, biased):
    tm = scores.shape[1]
    per = N_EXPERTS // N_GROUPS
    neg = jnp.full((), -jnp.inf, F32)
    b3 = biased.reshape(N_GROUPS, per, tm)
    m1, hit = _first_max(b3, 1, per)
    m2 = jnp.max(jnp.where(hit, neg, b3), axis=1, keepdims=True)
    gs = (m1 + m2).reshape(N_GROUPS, tm)
    gsel = jnp.zeros((N_GROUPS, tm), jnp.bool_)
    for _ in range(TOPK_GROUPS):
        _, hit = _first_max(gs, 0, N_GROUPS)
        gsel = jnp.logical_or(gsel, hit)
        gs = jnp.where(hit, neg, gs)
    emask = jnp.broadcast_to(gsel.reshape(N_GROUPS, 1, tm), (N_GROUPS, per, tm)).reshape(N_EXPERTS, tm)
    cand = jnp.where(emask, biased, neg)
    esel = jnp.zeros((N_EXPERTS, tm), jnp.bool_)
    for _ in range(TOP_K):
        _, hit = _first_max(cand, 0, N_EXPERTS)
        esel = jnp.logical_or(esel, hit)
        cand = jnp.where(hit, neg, cand)
    wsel = jnp.where(esel, scores, 0.0)
    return wsel / jnp.sum(wsel, axis=0, keepdims=True) * ROUTED_SCALE


def _post_kernel(o_ref, g_ref, bo_ref, ga_ref, bm_ref, x_ref,
                 gng_ref, gnb_ref, wo_ref, n2_ref, wsg_ref, wsu_ref, wsd_ref, wrt_ref, eb_ref,
                 base_ref, hn_ref, gate_ref):
    seg, exp = _seg_mats()
    tm = o_ref.shape[0]
    nparts = 2 if tm % (2 * LANES) == 0 else 1
    rows = [pl.ds(i * (tm // nparts), tm // nparts) for i in range(nparts)]
    inv_n = 1.0 / HEAD_DIM
    o = [o_ref[rs, :] for rs in rows]
    mean = [_mm_exact_rhs(_mm_exact_rhs(x, seg, passes=1) * inv_n, exp, passes=2) for x in o]
    d = _map(lambda x, m: x - m, o, mean)
    var = [_mm_exact_rhs(x * x, seg, passes=1) * inv_n for x in d]
    rstd = [_mm_exact_rhs(lax.rsqrt(x + GN_EPS), exp, passes=1) for x in var]
    merged = [ga_ref[rs, :] * ((x * r * gng_ref[...] + gnb_ref[...] + bo_ref[rs, :]) * g_ref[rs, :]) + bm_ref[rs, :]
              for rs, x, r in zip(rows, d, rstd)]
    h = [x_ref[rs, :] + _dg(m.astype(BF16), wo_ref[...], NN) for rs, m in zip(rows, merged)]
    hn = [_rmsnorm(x, n2_ref[...]) for x in h]
    hb = [x.astype(BF16) for x in hn]
    sg = [_dg(x, wsg_ref[...], NN) for x in hb]
    su = [_dg(x, wsu_ref[...], NN) for x in hb]
    logits = [_mm3(wrt_ref[...], x, NT) for x in hn]
    shared = _map(lambda a, b: _dg((a * _sigmoid(a) * b).astype(BF16), wsd_ref[...], NN), sg, su)
    for rs, x, sh, xb, lg in zip(rows, h, shared, hb, logits):
        base_ref[rs, :] = x + sh
        hn_ref[rs, :] = xb
        scores = _sigmoid(lg)
        gate_t = _route(scores, scores + eb_ref[...])
        gate_pad = jnp.concatenate([gate_t, jnp.zeros((LANES - N_EXPERTS, gate_t.shape[1]), F32)], axis=0)
        gate_ref[rs, :] = gate_pad.T


def _post(o, g, bonus, ga, bm, x, p, tm):
    N, D = x.shape
    tm = min(tm, N)
    params = (p['gn_g'], p['gn_b'], p['w_out'], p['norm2_g'], p['ws_gate'], p['ws_up'], p['ws_down'],
              p['w_router_t'], p['e_bias'])
    row = pl.BlockSpec((tm, D), lambda i: (i, 0))
    return pl.pallas_call(
        _post_kernel,
        grid=(N // tm,),
        in_specs=[row] * 6 + [_full(a.shape) for a in params],
        out_specs=[row, row, pl.BlockSpec((tm, LANES), lambda i: (i, 0))],
        out_shape=[jax.ShapeDtypeStruct((N, D), F32), jax.ShapeDtypeStruct((N, D), BF16),
                   jax.ShapeDtypeStruct((N, LANES), F32)],
        compiler_params=_cparams(("parallel",)),
        name="post",
    )(o, g, bonus, ga, bm, x, *params)


def _moe_kernel(x_ref, gate_ref, base_ref, wg_ref, wu_ref, wd_ref, nf_ref, y_ref):
    j = pl.program_id(1)
    nj = pl.num_programs(1)
    eps = wg_ref.shape[0]

    @pl.when(j == 0)
    def _():
        y_ref[...] = base_ref[...]

    gate = gate_ref[...]
    lane = lax.broadcasted_iota(jnp.int32, gate.shape, 1)
    x = x_ref[...]
    cols = [jnp.sum(jnp.where(lane == j * eps + q, gate, 0.0), axis=1, keepdims=True) for q in range(eps)]
    hg = [_dg(x, wg_ref[q], NN) for q in range(eps)]
    hu = [_dg(x, wu_ref[q], NN) for q in range(eps)]
    hh = [(hg[q] * _sigmoid(hg[q]) * hu[q] * cols[q]).astype(BF16) for q in range(eps)]
    y_ref[...] += _dg(jnp.concatenate(hh, axis=1), wd_ref[...].reshape(eps * D_EXPERT, D_MODEL), NN)

    @pl.when(j == nj - 1)
    def _():
        y_ref[...] = _rmsnorm(y_ref[...], nf_ref[...])


def _moe(hn, gate, base, p, tm, eps):
    N, D = base.shape
    tm = min(tm, N)
    row = lambda w: pl.BlockSpec((tm, w), lambda i, e: (i, 0))
    return pl.pallas_call(
        _moe_kernel,
        grid=(N // tm, N_EXPERTS // eps),
        in_specs=[row(D), row(LANES), row(D),
                  pl.BlockSpec((eps, D, D_EXPERT), lambda i, e: (e, 0, 0)),
                  pl.BlockSpec((eps, D, D_EXPERT), lambda i, e: (e, 0, 0)),
                  pl.BlockSpec((eps, D_EXPERT, D), lambda i, e: (e, 0, 0)),
                  pl.BlockSpec((1, D), lambda i, e: (0, 0))],
        out_specs=row(D),
        out_shape=jax.ShapeDtypeStruct((N, D), F32),
        compiler_params=_cparams(("parallel", "arbitrary")),
        name="moe",
    )(hn, gate, base, p['w_gate'], p['w_up'], p['w_down'], p['normf_g'])


def kernel(x_prompt, x_sample, state_wkv, state_shift, state_conv, norm1_g, w_in, mu_shift, w0, w_decay_up, a0, a_up, g_up, k_k, k_a, r_k, gn_g, gn_b, conv_w, conv_b, cln_g, cln_b, w_out, norm2_g, w_router, e_bias, w_gate, w_up, w_down, ws_gate, ws_up, ws_down, normf_g):
    depth = w_in.shape[0]
    assert depth == 1
    B, T, D = x_prompt.shape
    NS = x_sample.shape[0]
    assert x_sample.shape[1] == 1 and D == D_MODEL and T % CHUNK == 0
    row = lambda a: a[0].reshape(1, -1)
    p = {
        'norm1_g': row(norm1_g), 'mu_shift': row(mu_shift), 'w0': row(w0), 'a0': row(a0),
        'k_k': row(k_k), 'k_a': row(k_a), 'r_k': row(r_k), 'gn_g': row(gn_g), 'gn_b': row(gn_b),
        'conv_b': row(conv_b), 'cln_g': row(cln_g), 'cln_b': row(cln_b), 'norm2_g': row(norm2_g),
        'normf_g': normf_g.reshape(1, -1),
        'w_in': w_in[0].astype(BF16),
        'w_decay_up': w_decay_up[0], 'a_up': a_up[0], 'g_up': g_up[0], 'conv_w': conv_w[0],
        'w_out': w_out[0].astype(BF16),
        'ws_gate': ws_gate[0].astype(BF16), 'ws_up': ws_up[0].astype(BF16), 'ws_down': ws_down[0].astype(BF16),
        'w_router_t': w_router[0].T, 'e_bias': e_bias[0].reshape(-1, 1),
        'w_gate': w_gate[0].astype(BF16), 'w_up': w_up[0].astype(BF16), 'w_down': w_down[0].astype(BF16),
    }

    zp0 = jnp.zeros((B, 1, C_RWKV), F32)
    r, lw, k, v, kk, b, g, bonus, shift_p, ga, bm, conv_p = _mix_seq(
        x_prompt, zp0, jnp.zeros((B, CONV_WIDTH - 1, D), F32), p, tm=TM_MIX)
    o, wkv_p = _wkv(r, lw, k, v, kk, b, rows=min(ROWS_WKV, T))
    flat = lambda a: a.reshape(B * T, D)
    base, hn, gate = _post(flat(o), flat(g), flat(bonus), flat(ga), flat(bm), flat(x_prompt), p, tm=TM_POST)
    y_prompt = _moe(hn, gate, base, p, tm=TM_MOE, eps=EXPERTS_PER_STEP).reshape(B, T, D)

    xs = x_sample.reshape(NS, D)
    r, lw, k, v, kk, b, g, bonus, shift_s = _rwkv_prep_batch(xs, state_shift[0], p)
    ga, bm, conv_s_t = _conv_branch_batch(xs, jnp.swapaxes(state_conv[0], 0, 1), p)
    o_t, wkv_s_t = _wkv_step(jnp.transpose(state_wkv[0], (1, 2, 3, 0)),
                             jnp.transpose(jnp.stack([r, lw, k, v, kk, b]), (0, 2, 1)))
    o = o_t.T
    wkv_s = jnp.transpose(wkv_s_t, (3, 0, 1, 2))
    base, hn, gate = _post(o, g, bonus, ga, bm, xs, p, tm=TM_POST)
    y_sample = _moe(hn, gate, base, p, tm=TM_MOE, eps=EXPERTS_PER_STEP).reshape(NS, 1, D)

    return (y_prompt, y_sample, wkv_p[None], shift_p.reshape(1, B, D), conv_p[None],
            wkv_s[None], shift_s[None], jnp.swapaxes(conv_s_t, 0, 1)[None])
```
